```python
import jax, jax.numpy as jnp
from jax import lax
import numpy as np

D_MODEL = 1024
BATCH = 32
SEQ = 2048
DEPTH = 2

N_A = DEPTH // 2
N_B = DEPTH - N_A
GLA_HEADS = 4
GLA_DK = D_MODEL // 2 // GLA_HEADS
GLA_DV = D_MODEL // GLA_HEADS
GLA_GATE_RANK = 16
GLA_GATE_TAU = 16.0
GLA_CHUNK = 64
GLA_IN = 2 * GLA_HEADS * GLA_DK + 2 * GLA_HEADS * GLA_DV + GLA_GATE_RANK
FOX_HEADS = 16
FOX_HD = D_MODEL // FOX_HEADS
Q_BLOCK = 128
KV_IN = 2 * D_MODEL + FOX_HEADS
D_FF = 2816
CONV_W = 3
PLE_DIM = 256
EPS = 1e-6

kernel_name = "yoco_gla_fox_convffn_ple"


def _rmsnorm(x, g):
    xf = x.astype(jnp.float32)
    y = xf * lax.rsqrt(jnp.mean(xf * xf, axis=-1, keepdims=True) + EPS)
    return (y * g.astype(jnp.float32)).astype(x.dtype)


def _gla_mixer(h, w_in, w_gk2, b_gk2, g_onorm, w_out):
    B, S, _ = h.shape
    H, DK, DV, C = GLA_HEADS, GLA_DK, GLA_DV, GLA_CHUNK
    nc = S // C
    proj = h @ w_in
    q, k, v, og, gr = jnp.split(proj, [H * DK, 2 * H * DK, 2 * H * DK + H * DV,
                                       2 * H * DK + 2 * H * DV], axis=-1)
    gk = jax.nn.log_sigmoid((gr @ w_gk2 + b_gk2).astype(jnp.float32)) / GLA_GATE_TAU

    def chunks(t, d):
        return t.reshape(B, nc, C, H, d).transpose(0, 3, 1, 2, 4)

    q = chunks(q, DK) * (DK ** -0.5)
    k = chunks(k, DK)
    v = chunks(v, DV)
    bcum = jnp.cumsum(chunks(gk, DK), axis=3)
    b_last = bcum[:, :, :, -1:, :]
    q_in = q * jnp.exp(bcum)
    k_in = k * jnp.exp(-bcum)
    k_end = k * jnp.exp(b_last - bcum)
    decay = jnp.exp(b_last[:, :, :, 0, :])

    att = jnp.einsum('bhncd,bhnsd->bhncs', q_in, k_in)
    causal = jnp.tril(jnp.ones((C, C), dtype=bool))
    att = jnp.where(causal, att, 0.0)
    o_intra = jnp.einsum('bhncs,bhnsv->bhncv', att, v)

    def step(state, xs):
        q_n, k_n, v_n, dec_n = xs
        o_n = jnp.einsum('bhcd,bhdv->bhcv', q_n, state)
        state = dec_n[..., None] * state + jnp.einsum('bhcd,bhcv->bhdv', k_n, v_n)
        return state, o_n

    s0 = jnp.zeros((B, H, DK, DV), jnp.float32)
    xs = (jnp.moveaxis(q_in, 2, 0), jnp.moveaxis(k_end, 2, 0),
          jnp.moveaxis(v, 2, 0).astype(jnp.float32), jnp.moveaxis(decay, 2, 0))
    _, o_inter = lax.scan(step, s0, xs)
    o = o_intra + jnp.moveaxis(o_inter, 0, 2)
    o = o.transpose(0, 2, 3, 1, 4).reshape(B, S, H, DV).astype(h.dtype)
    o = _rmsnorm(o, g_onorm).reshape(B, S, H * DV) * jax.nn.silu(og)
    return o @ w_out


def _shared_kv(x, g_norm, w_in, g_knorm, b_f):
    B, S, _ = x.shape
    H, hd = FOX_HEADS, FOX_HD
    h = _rmsnorm(x, g_norm)
    k, v, f = jnp.split(h @ w_in, [D_MODEL, 2 * D_MODEL], axis=-1)
    k = _rmsnorm(k.reshape(B, S, H, hd), g_knorm).transpose(0, 2, 1, 3)
    v = v.reshape(B, S, H, hd).transpose(0, 2, 1, 3)
    log_f = jax.nn.log_sigmoid(f.astype(jnp.float32) + b_f.astype(jnp.float32))
    c = jnp.cumsum(log_f, axis=1).transpose(0, 2, 1)
    return k, v, c


def _fox_mixer(h, w_in, g_qnorm, w_out, k, v, c):
    B, S, _ = h.shape
    H, hd = FOX_HEADS, FOX_HD
    q, og = jnp.split(h @ w_in, [D_MODEL], axis=-1)
    q = _rmsnorm(q.reshape(B, S, H, hd), g_qnorm).transpose(0, 2, 1, 3) * (hd ** -0.5)
    outs = []
    for i in range(S // Q_BLOCK):
        lo, hi = i * Q_BLOCK, (i + 1) * Q_BLOCK
        qb = q[:, :, lo:hi]
        kb = k[:, :, :hi]
        vb = v[:, :, :hi]
        logits = (jnp.einsum('bhqd,bhkd->bhqk', qb, kb).astype(jnp.float32)
                  + c[:, :, lo:hi, None] - c[:, :, None, :hi])
        mask = (lo + jnp.arange(Q_BLOCK))[:, None] >= jnp.arange(hi)[None, :]
        logits = jnp.where(mask, logits, -jnp.inf)
        probs = jax.nn.softmax(logits, axis=-1).astype(vb.dtype)
        outs.append(jnp.einsum('bhqk,bhkd->bhqd', probs, vb))
    o = jnp.concatenate(outs, axis=2).transpose(0, 2, 1, 3).reshape(B, S, D_MODEL)
    return (o * jax.nn.sigmoid(og)) @ w_out


def _conv_ffn(h, w_up, w_dw, b_dw, w_down):
    S = h.shape[1]
    u = h @ w_up
    up = jnp.pad(u, ((0, 0), (CONV_W - 1, 0), (0, 0)))
    u = sum(w_dw[j] * up[:, j:j + S] for j in range(CONV_W)) + b_dw
    a, b = jnp.split(u, [D_FF], axis=-1)
    return (jax.nn.silu(a) * b) @ w_down


def _ple(x, p_i, g, w_gate, b_gate, w_proj):
    gate = jax.nn.sigmoid(_rmsnorm(x, g) @ w_gate + b_gate)
    return (p_i @ w_proj) * gate


def setup_inputs(seed: int = 0) -> dict:
    key = jax.random.key(seed)
    ks = iter(jax.random.split(key, 32))

    def nrm(shape, scale):
        return jax.random.normal(next(ks), shape, jnp.float32) * scale

    def gain(shape):
        return 1.0 + 0.05 * jax.random.normal(next(ks), shape, jnp.float32)

    D, H = D_MODEL, FOX_HEADS
    return {
        "x": nrm((BATCH, SEQ, D), 1.0),
        "p": nrm((DEPTH, BATCH, SEQ, PLE_DIM), 1.0),
        "g_mix": gain((DEPTH, D)),
        "g_ffn": gain((DEPTH, D)),
        "g_ple": gain((DEPTH, D)),
        "gla_w_in": nrm((N_A, D, GLA_IN), D ** -0.5),
        "gla_w_gk2": nrm((N_A, GLA_GATE_RANK, GLA_HEADS * GLA_DK), GLA_GATE_RANK ** -0.5),
        "gla_b_gk2": nrm((N_A, GLA_HEADS * GLA_DK), 0.1),
        "gla_g_onorm": gain((N_A, GLA_DV)),
        "gla_w_out": nrm((N_A, GLA_HEADS * GLA_DV, D), (GLA_HEADS * GLA_DV) ** -0.5),
        "kv_g_norm": gain((D,)),
        "kv_w_in": nrm((D, KV_IN), D ** -0.5),
        "kv_g_knorm": gain((FOX_HD,)),
        "kv_b_f": 1.0 + 4.0 * jax.random.uniform(next(ks), (H,), jnp.float32),
        "fox_w_in": nrm((N_B, D, 2 * D), D ** -0.5),
        "fox_g_qnorm": gain((N_B, FOX_HD)),
        "fox_w_out": nrm((N_B, D, D), D ** -0.5),
        "ffn_w_up": nrm((DEPTH, D, 2 * D_FF), D ** -0.5),
        "ffn_w_dw": nrm((DEPTH, CONV_W, 2 * D_FF), CONV_W ** -0.5),
        "ffn_b_dw": nrm((DEPTH, 2 * D_FF), 0.02),
        "ffn_w_down": nrm((DEPTH, D_FF, D), D_FF ** -0.5),
        "ple_w_gate": nrm((DEPTH, D, D), D ** -0.5),
        "ple_b_gate": nrm((DEPTH, D), 0.02),
        "ple_w_proj": nrm((DEPTH, PLE_DIM, D), PLE_DIM ** -0.5),
        "g_final": gain((D,)),
    }


def reference(x, p, g_mix, g_ffn, g_ple, gla_w_in, gla_w_gk2, gla_b_gk2, gla_g_onorm,
              gla_w_out, kv_g_norm, kv_w_in, kv_g_knorm, kv_b_f, fox_w_in, fox_g_qnorm,
              fox_w_out, ffn_w_up, ffn_w_dw, ffn_b_dw, ffn_w_down, ple_w_gate, ple_b_gate,
              ple_w_proj, g_final):
    k_sh = v_sh = c_sh = None
    for i in range(DEPTH):
        if i == N_A:
            k_sh, v_sh, c_sh = _shared_kv(x, kv_g_norm, kv_w_in, kv_g_knorm, kv_b_f)
        h = _rmsnorm(x, g_mix[i])
        if i < N_A:
            x = x + _gla_mixer(h, gla_w_in[i], gla_w_gk2[i], gla_b_gk2[i],
                               gla_g_onorm[i], gla_w_out[i])
        else:
            j = i - N_A
            x = x + _fox_mixer(h, fox_w_in[j], fox_g_qnorm[j], fox_w_out[j], k_sh, v_sh, c_sh)
        x = x + _conv_ffn(_rmsnorm(x, g_ffn[i]), ffn_w_up[i], ffn_w_dw[i], ffn_b_dw[i],
                          ffn_w_down[i])
        x = x + _ple(x, p[i], g_ple[i], ple_w_gate[i], ple_b_gate[i], ple_w_proj[i])
    return _rmsnorm(x, g_final)
```

```python
import functools

import jax
import jax.numpy as jnp
from jax import lax
from jax.experimental import pallas as pl
from jax.experimental.pallas import tpu as pltpu

F32 = jnp.float32
BF16 = jnp.bfloat16

EPS = 1e-6
D_MODEL = 1024
GLA_HEADS = 4
GLA_DK = 128
GLA_DV = 256
GLA_QK = GLA_HEADS * GLA_DK
GLA_V = GLA_HEADS * GLA_DV
GLA_RANK = 16
GLA_TAU = 16.0
GLA_CHUNK = 64
FOX_HEADS = 16
FOX_HD = 64
D_FF = 2816
PLE_DIM = 256

LANES = 128
SUBLANES = 8
VMEM_LIMIT = 56 * 1024 * 1024

TM_PROJ = 512
TM_GLA = 512
TM_FFN = 512
FFN_CHUNK = 256
TQ = 256
TK = 256


def _dot(a, b):
    return jnp.dot(a, b, preferred_element_type=F32)


def _dot_nt(a, b):
    return lax.dot_general(a, b, (((1,), (1,)), ((), ())), preferred_element_type=F32)


def _dot_tn(a, b):
    return lax.dot_general(a, b, (((0,), (0,)), ((), ())), preferred_element_type=F32)


def _split3(x):
    hi = x.astype(BF16)
    r = x - hi.astype(F32)
    mid = r.astype(BF16)
    lo = (r - mid.astype(F32)).astype(BF16)
    return hi, mid, lo


def _rms(x, g):
    return x * lax.rsqrt(jnp.mean(x * x, axis=-1, keepdims=True) + EPS) * g


def _sigmoid(z):
    return 1.0 / (1.0 + jnp.exp(-z))


def _log_sigmoid(z):
    return jnp.minimum(z, 0.0) - jnp.log1p(jnp.exp(-jnp.abs(z)))


def _const_spec(shape):
    nd = len(shape)
    return pl.BlockSpec(shape, lambda *_: (0,) * nd, pipeline_mode=pl.Buffered(1))


def _params(sem):
    return pltpu.CompilerParams(dimension_semantics=sem, vmem_limit_bytes=VMEM_LIMIT)


def _gla_proj_kernel(x_ref, g_ref, w_ref, wg2_ref, bg2_ref, q_ref, k_ref, v_ref, og_ref, gk_ref):
    hn = _rms(x_ref[...], g_ref[...]).astype(BF16)
    q_ref[...] = _dot(hn, w_ref[:, 0:GLA_QK]).astype(BF16)
    k_ref[...] = _dot(hn, w_ref[:, GLA_QK:2 * GLA_QK]).astype(BF16)
    v_ref[...] = _dot(hn, w_ref[:, 2 * GLA_QK:2 * GLA_QK + GLA_V]).astype(BF16)
    og_ref[...] = _dot(hn, w_ref[:, 2 * GLA_QK + GLA_V:2 * GLA_QK + 2 * GLA_V]).astype(BF16)
    gr = _dot(hn, w_ref[:, 2 * GLA_QK + 2 * GLA_V:]).astype(BF16)
    z = _dot(gr, wg2_ref[...]) + bg2_ref[...]
    gk_ref[...] = _log_sigmoid(z) * (1.0 / GLA_TAU)


def _gla_proj(x2d, g, w_in, w_gk2, b_gk2):
    t = x2d.shape[0]
    tm = TM_PROJ
    n_main = 2 * GLA_QK + 2 * GLA_V
    w_pad = jnp.zeros((D_MODEL, n_main + LANES), BF16).at[:, :n_main + GLA_RANK].set(w_in.astype(BF16))
    wg2_pad = jnp.zeros((LANES, GLA_QK), BF16).at[:GLA_RANK].set(w_gk2.astype(BF16))
    row = lambda i: (i, 0)
    return pl.pallas_call(
        _gla_proj_kernel,
        grid=(t // tm,),
        in_specs=[
            pl.BlockSpec((tm, D_MODEL), row),
            _const_spec((1, D_MODEL)),
            _const_spec((D_MODEL, n_main + LANES)),
            _const_spec((LANES, GLA_QK)),
            _const_spec((1, GLA_QK)),
        ],
        out_specs=[
            pl.BlockSpec((tm, GLA_QK), row),
            pl.BlockSpec((tm, GLA_QK), row),
            pl.BlockSpec((tm, GLA_V), row),
            pl.BlockSpec((tm, GLA_V), row),
            pl.BlockSpec((tm, GLA_QK), row),
        ],
        out_shape=[
            jax.ShapeDtypeStruct((t, GLA_QK), BF16),
            jax.ShapeDtypeStruct((t, GLA_QK), BF16),
            jax.ShapeDtypeStruct((t, GLA_V), BF16),
            jax.ShapeDtypeStruct((t, GLA_V), BF16),
            jax.ShapeDtypeStruct((t, GLA_QK), F32),
        ],
        compiler_params=_params(("parallel",)),
        name="gla_proj",
    )(x2d, g.reshape(1, D_MODEL), w_pad, wg2_pad, b_gk2.reshape(1, GLA_QK))


def _gla_scan_kernel(q_ref, k_ref, v_ref, og_ref, gk_ref, x_ref, gon_ref, wout_ref, out_ref,
                     state_ref, o_ref):
    c = GLA_CHUNK
    tm = x_ref.shape[0]

    @pl.when(pl.program_id(1) == 0)
    def _():
        state_ref[...] = jnp.zeros_like(state_ref)

    rowi = lax.broadcasted_iota(jnp.int32, (c, c), 0)
    coli = lax.broadcasted_iota(jnp.int32, (c, c), 1)
    tril = rowi >= coli
    tril_bf = jnp.where(tril, 1.0, 0.0).astype(BF16)
    scale = GLA_DK ** -0.5

    def chunk(ci, carry):
        r0 = pl.multiple_of(ci * c, c)
        gk = gk_ref[pl.ds(r0, c), :]
        hi, mid, lo = _split3(gk)
        bcum = _dot(tril_bf, hi) + _dot(tril_bf, mid) + _dot(tril_bf, lo)
        b_last = bcum[c - 1:c, :]
        q = q_ref[pl.ds(r0, c), :].astype(F32) * scale
        k = k_ref[pl.ds(r0, c), :].astype(F32)
        q_in = (q * jnp.exp(bcum)).astype(BF16)
        k_in = (k * jnp.exp(-bcum)).astype(BF16)
        k_end = (k * jnp.exp(b_last - bcum)).astype(BF16)
        decay = jnp.exp(b_last)
        for h in range(GLA_HEADS):
            ks = slice(h * GLA_DK, (h + 1) * GLA_DK)
            vs = slice(h * GLA_DV, (h + 1) * GLA_DV)
            att = _dot_nt(q_in[:, ks], k_in[:, ks])
            att = jnp.where(tril, att, 0.0).astype(BF16)
            v_h = v_ref[pl.ds(r0, c), vs]
            st = state_ref[h]
            o_ref[pl.ds(r0, c), vs] = _dot(att, v_h) + _dot(q_in[:, ks], st.astype(BF16))
            dcol = jnp.broadcast_to(decay[:, ks], (LANES, GLA_DK)).T[:, 0:1]
            state_ref[h] = dcol * st + _dot_tn(k_end[:, ks], v_h)
        return carry

    lax.fori_loop(0, tm // c, chunk, 0)

    y = x_ref[...]
    for h in range(GLA_HEADS):
        vs = slice(h * GLA_DV, (h + 1) * GLA_DV)
        on = _rms(o_ref[:, vs], gon_ref[...])
        og = og_ref[:, vs].astype(F32)
        gated = (on * (og * _sigmoid(og))).astype(BF16)
        y = y + _dot(gated, wout_ref[vs, :])
    out_ref[...] = y


def _gla_scan(q, k, v, og, gk, x2d, g_onorm, w_out, batch, seq):
    tm = TM_GLA
    nt = seq // tm
    row = lambda b, s: (b * nt + s, 0)
    return pl.pallas_call(
        _gla_scan_kernel,
        grid=(batch, nt),
        in_specs=[
            pl.BlockSpec((tm, GLA_QK), row),
            pl.BlockSpec((tm, GLA_QK), row),
            pl.BlockSpec((tm, GLA_V), row),
            pl.BlockSpec((tm, GLA_V), row),
            pl.BlockSpec((tm, GLA_QK), row),
            pl.BlockSpec((tm, D_MODEL), row),
            _const_spec((1, GLA_DV)),
            _const_spec((GLA_V, D_MODEL)),
        ],
        out_specs=pl.BlockSpec((tm, D_MODEL), row),
        out_shape=jax.ShapeDtypeStruct(x2d.shape, F32),
        scratch_shapes=[
            pltpu.VMEM((GLA_HEADS, GLA_DK, GLA_DV), F32),
            pltpu.VMEM((tm, GLA_V), F32),
        ],
        compiler_params=_params(("parallel", "arbitrary")),
        name="gla_scan",
    )(q, k, v, og, gk, x2d, g_onorm.reshape(1, GLA_DV), w_out.astype(BF16))


def _ffn_kernel(x_ref, p_ref, gf_ref, wup_ref, wdw_ref, bdw_ref, wdn_ref, gp_ref, wg_ref, bg_ref,
                wp_ref, gfin_ref, out_ref, carry_ref, *, final_norm):
    tm = x_ref.shape[0]
    fc = FFN_CHUNK
    nch = D_FF // fc

    @pl.when(pl.program_id(1) == 0)
    def _():
        carry_ref[...] = jnp.zeros_like(carry_ref)

    x = x_ref[...]
    hn = _rms(x, gf_ref[...]).astype(BF16)
    rowid = lax.broadcasted_iota(jnp.int32, (SUBLANES, fc), 0)

    def conv(col0, slot):
        u = _dot(hn, wup_ref[:, col0:col0 + fc])
        prev = carry_ref[slot]
        carry_ref[slot] = u[tm - SUBLANES:tm, :]
        r1 = pltpu.roll(u, 1, 0)
        r2 = pltpu.roll(u, 2, 0)
        p6 = prev[SUBLANES - 2:SUBLANES - 1, :]
        p7 = prev[SUBLANES - 1:SUBLANES, :]
        top1 = jnp.where(rowid == 0, p7, r1[0:SUBLANES])
        top2 = jnp.where(rowid == 0, p6, jnp.where(rowid == 1, p7, r2[0:SUBLANES]))
        u1 = jnp.concatenate([top1, r1[SUBLANES:]], axis=0)
        u2 = jnp.concatenate([top2, r2[SUBLANES:]], axis=0)
        w = wdw_ref[:, col0:col0 + fc]
        return w[0:1] * u2 + w[1:2] * u1 + w[2:3] * u + bdw_ref[:, col0:col0 + fc]

    y = x
    for ci in range(nch):
        a = conv(ci * fc, ci)
        b = conv(D_FF + ci * fc, nch + ci)
        act = (a * _sigmoid(a) * b).astype(BF16)
        y = y + _dot(act, wdn_ref[ci * fc:(ci + 1) * fc, :])

    hp = _rms(y, gp_ref[...]).astype(BF16)
    gate = _sigmoid(_dot(hp, wg_ref[...]) + bg_ref[...])
    pe = _dot(p_ref[...].astype(BF16), wp_ref[...])
    y = y + pe * gate
    if final_norm:
        y = _rms(y, gfin_ref[...])
    out_ref[...] = y


def _ffn_ple(x2d, p2d, g_ffn, w_up, w_dw, b_dw, w_down, g_ple, w_gate, b_gate, w_proj, g_final,
             batch, seq, final_norm):
    tm = TM_FFN
    nt = seq // tm
    nslots = 2 * (D_FF // FFN_CHUNK)
    row = lambda b, s: (b * nt + s, 0)
    return pl.pallas_call(
        functools.partial(_ffn_kernel, final_norm=final_norm),
        grid=(batch, nt),
        in_specs=[
            pl.BlockSpec((tm, D_MODEL), row),
            pl.BlockSpec((tm, PLE_DIM), row),
            _const_spec((1, D_MODEL)),
            _const_spec((D_MODEL, 2 * D_FF)),
            _const_spec((3, 2 * D_FF)),
            _const_spec((1, 2 * D_FF)),
            _const_spec((D_FF, D_MODEL)),
            _const_spec((1, D_MODEL)),
            _const_spec((D_MODEL, D_MODEL)),
            _const_spec((1, D_MODEL)),
            _const_spec((PLE_DIM, D_MODEL)),
            _const_spec((1, D_MODEL)),
        ],
        out_specs=pl.BlockSpec((tm, D_MODEL), row),
        out_shape=jax.ShapeDtypeStruct(x2d.shape, F32),
        scratch_shapes=[pltpu.VMEM((nslots, SUBLANES, FFN_CHUNK), F32)],
        compiler_params=_params(("parallel", "arbitrary")),
        name="ffn_ple",
    )(x2d, p2d, g_ffn.reshape(1, -1), w_up.astype(BF16), w_dw, b_dw.reshape(1, -1),
      w_down.astype(BF16), g_ple.reshape(1, -1), w_gate.astype(BF16), b_gate.reshape(1, -1),
      w_proj.astype(BF16), g_final.reshape(1, -1))


def _fox_proj_kernel(x_ref, gm_ref, gkv_ref, wq_ref, wog_ref, wk_ref, wv_ref, wf_ref, wft_ref,
                     bf_ref, bfc_ref, gq_ref, gkn_ref, ones_ref, tril_ref, triu_ref,
                     q_ref, sg_ref, k_ref, v_ref, c_ref, ct_ref, crow_ref, ccol_ref):
    tm = x_ref.shape[0]

    @pl.when(pl.program_id(1) == 0)
    def _():
        crow_ref[...] = jnp.zeros_like(crow_ref)
        ccol_ref[...] = jnp.zeros_like(ccol_ref)

    x = x_ref[...]
    xn = x * lax.rsqrt(jnp.mean(x * x, axis=-1, keepdims=True) + EPS)
    h1 = (xn * gm_ref[...]).astype(BF16)
    h2 = (xn * gkv_ref[...]).astype(BF16)

    def head_norm(t, g):
        t2 = (t * t).astype(BF16)
        blk = 2 * LANES
        ss = jnp.concatenate(
            [_dot(t2[:, i * blk:(i + 1) * blk], ones_ref[...]) for i in range(D_MODEL // blk)], axis=1)
        return t * lax.rsqrt(ss * (1.0 / FOX_HD) + EPS) * g

    q = _dot(h1, wq_ref[...])
    q_ref[...] = (head_norm(q, gq_ref[...]) * (FOX_HD ** -0.5)).astype(BF16)
    sg_ref[...] = _sigmoid(_dot(h1, wog_ref[...])).astype(BF16)
    k = _dot(h2, wk_ref[...])
    k_ref[...] = head_norm(k, gkn_ref[...]).astype(BF16)
    v_ref[...] = _dot(h2, wv_ref[...]).astype(BF16)

    lf = _log_sigmoid(_dot(h2, wf_ref[...]) + bf_ref[...])
    hi, mid, lo = _split3(lf)
    tril = tril_ref[...]
    c = _dot(tril, hi) + _dot(tril, mid) + _dot(tril, lo) + crow_ref[0:1, :]
    c_ref[...] = c
    crow_ref[...] = jnp.broadcast_to(c[tm - 1:tm, :], crow_ref.shape)

    lft = _log_sigmoid(_dot_nt(wft_ref[...], h2) + bfc_ref[:, 0:1])
    hi, mid, lo = _split3(lft)
    triu = triu_ref[...]
    ct = _dot(hi, triu) + _dot(mid, triu) + _dot(lo, triu) + ccol_ref[:, 0:1]
    for i in range(tm // TK):
        ct_ref[0, i] = ct[:, i * TK:(i + 1) * TK]
    ccol_ref[...] = jnp.broadcast_to(ct[:, tm - 1:tm], ccol_ref.shape)


def _fox_proj(x2d, g_mix, g_kv, w_fox_in, w_kv_in, b_f, g_qnorm, g_knorm, batch, seq):
    tm = TM_PROJ
    nt = seq // tm
    d = D_MODEL
    wq = w_fox_in[:, :d].astype(BF16)
    wog = w_fox_in[:, d:].astype(BF16)
    wk = w_kv_in[:, :d].astype(BF16)
    wv = w_kv_in[:, d:2 * d].astype(BF16)
    wf = w_kv_in[:, 2 * d:]
    wf_pad = jnp.zeros((d, LANES), BF16).at[:, :FOX_HEADS].set(wf.astype(BF16))
    wft = wf.T.astype(BF16)
    bf_pad = jnp.zeros((1, LANES), F32).at[0, :FOX_HEADS].set(b_f)
    bf_col = jnp.broadcast_to(b_f[:, None], (FOX_HEADS, LANES))
    gq = jnp.tile(g_qnorm, FOX_HEADS).reshape(1, d)
    gkn = jnp.tile(g_knorm, FOX_HEADS).reshape(1, d)
    blk = 2 * LANES
    hid = jnp.arange(blk) // FOX_HD
    ones_bd = (hid[:, None] == hid[None, :]).astype(BF16)
    tid = jnp.arange(tm)
    tril = (tid[:, None] >= tid[None, :]).astype(BF16)
    triu = tril.T
    row = lambda b, s: (b * nt + s, 0)
    return pl.pallas_call(
        _fox_proj_kernel,
        grid=(batch, nt),
        in_specs=[
            pl.BlockSpec((tm, d), row),
            _const_spec((1, d)), _const_spec((1, d)),
            _const_spec((d, d)), _const_spec((d, d)), _const_spec((d, d)), _const_spec((d, d)),
            _const_spec((d, LANES)), _const_spec((FOX_HEADS, d)),
            _const_spec((1, LANES)), _const_spec((FOX_HEADS, LANES)),
            _const_spec((1, d)), _const_spec((1, d)),
            _const_spec((blk, blk)), _const_spec((tm, tm)), _const_spec((tm, tm)),
        ],
        out_specs=[
            pl.BlockSpec((tm, d), row),
            pl.BlockSpec((tm, d), row),
            pl.BlockSpec((tm, d), row),
            pl.BlockSpec((tm, d), row),
            pl.BlockSpec((tm, LANES), row),
            pl.BlockSpec((1, tm // TK, FOX_HEADS, TK), lambda b, s: (b, s, 0, 0)),
        ],
        out_shape=[
            jax.ShapeDtypeStruct((batch * seq, d), BF16),
            jax.ShapeDtypeStruct((batch * seq, d), BF16),
            jax.ShapeDtypeStruct((batch * seq, d), BF16),
            jax.ShapeDtypeStruct((batch * seq, d), BF16),
            jax.ShapeDtypeStruct((batch * seq, LANES), F32),
            jax.ShapeDtypeStruct((batch, seq // TK, FOX_HEADS, TK), F32),
        ],
        scratch_shapes=[pltpu.VMEM((SUBLANES, LANES), F32), pltpu.VMEM((FOX_HEADS, LANES), F32)],
        compiler_params=_params(("parallel", "arbitrary")),
        name="fox_proj",
    )(x2d, g_mix.reshape(1, d), g_kv.reshape(1, d), wq, wog, wk, wv, wf_pad, wft, bf_pad, bf_col,
      gq, gkn, ones_bd, tril, triu)


def _fox_attn_kernel(q_ref, k_ref, v_ref, c_ref, ct_ref, sg_ref, x_ref, wout_ref, out_ref,
                     o_ref, acc_ref):
    qi = pl.program_id(1)
    low = lax.broadcasted_iota(jnp.int32, (TQ, LANES), 1) < FOX_HD
    causal = (lax.broadcasted_iota(jnp.int32, (TQ, TK), 0)
              >= lax.broadcasted_iota(jnp.int32, (TQ, TK), 1))
    neg_inf = jnp.full((TQ, 1), -jnp.inf, F32)
    zeros1 = jnp.zeros((TQ, 1), F32)

    for j in range(FOX_HEADS // 2):
        ls = slice(j * LANES, (j + 1) * LANES)
        qp = q_ref[:, ls]
        zero = jnp.zeros_like(qp)
        qm = (jnp.where(low, qp, zero), jnp.where(low, zero, qp))
        cq = (c_ref[:, 2 * j:2 * j + 1], c_ref[:, 2 * j + 1:2 * j + 2])
        acc_ref[...] = jnp.zeros_like(acc_ref)

        def body(kb, carry, masked):
            r0 = pl.multiple_of(kb * TK, TK)
            kp = k_ref[pl.ds(r0, TK), ls]
            vp = v_ref[pl.ds(r0, TK), ls]
            new = []
            for hh in range(2):
                h = 2 * j + hh
                s = _dot_nt(qm[hh], kp)
                s = s + (cq[hh] - ct_ref[0, kb, h:h + 1, :])
                if masked:
                    s = jnp.where(causal, s, -jnp.inf)
                m_old, l_old = carry[2 * hh], carry[2 * hh + 1]
                m_new = jnp.maximum(m_old, jnp.max(s, axis=-1, keepdims=True))
                alpha = jnp.exp(m_old - m_new)
                p = jnp.exp(s - m_new)
                l_new = alpha * l_old + jnp.sum(p, axis=-1, keepdims=True)
                acc_ref[hh] = alpha * acc_ref[hh] + _dot(p.astype(BF16), vp)
                new += [m_new, l_new]
            return tuple(new)

        carry = lax.fori_loop(0, qi, functools.partial(body, masked=False),
                              (neg_inf, zeros1, neg_inf, zeros1))
        _, l0, _, l1 = body(qi, carry, True)
        o_ref[:, ls] = jnp.where(low, acc_ref[0] / l0, acc_ref[1] / l1)

    gated = (o_ref[...] * sg_ref[...].astype(F32)).astype(BF16)
    out_ref[...] = x_ref[...] + _dot(gated, wout_ref[...])


def _fox_attn(q, k, v, c, ct, sg, x2d, w_out, batch, seq):
    nq = seq // TQ
    d = D_MODEL
    row = lambda b, s: (b * nq + s, 0)
    full = lambda b, s: (b, 0)
    return pl.pallas_call(
        _fox_attn_kernel,
        grid=(batch, nq),
        in_specs=[
            pl.BlockSpec((TQ, d), row),
            pl.BlockSpec((seq, d), full),
            pl.BlockSpec((seq, d), full),
            pl.BlockSpec((TQ, LANES), row),
            pl.BlockSpec((1, seq // TK, FOX_HEADS, TK), lambda b, s: (b, 0, 0, 0)),
            pl.BlockSpec((TQ, d), row),
            pl.BlockSpec((TQ, d), row),
            _const_spec((d, d)),
        ],
        out_specs=pl.BlockSpec((TQ, d), row),
        out_shape=jax.ShapeDtypeStruct(x2d.shape, F32),
        scratch_shapes=[pltpu.VMEM((TQ, d), F32), pltpu.VMEM((2, TQ, LANES), F32)],
        compiler_params=_params(("parallel", "arbitrary")),
        name="fox_attn",
    )(q, k, v, c, ct, sg, x2d, w_out.astype(BF16))


def kernel(x, p, g_mix, g_ffn, g_ple, gla_w_in, gla_w_gk2, gla_b_gk2, gla_g_onorm, gla_w_out,
           kv_g_norm, kv_w_in, kv_g_knorm, kv_b_f, fox_w_in, fox_g_qnorm, fox_w_out, ffn_w_up,
           ffn_w_dw, ffn_b_dw, ffn_w_down, ple_w_gate, ple_b_gate, ple_w_proj, g_final):
    batch, seq, d = x.shape
    depth = p.shape[0]
    n_gla = gla_w_in.shape[0]
    x2d = x.reshape(batch * seq, d)
    p2d = p.reshape(depth, batch * seq, PLE_DIM)
    for i in range(depth):
        if i < n_gla:
            q, k, v, og, gk = _gla_proj(x2d, g_mix[i], gla_w_in[i], gla_w_gk2[i], gla_b_gk2[i])
            x2d = _gla_scan(q, k, v, og, gk, x2d, gla_g_onorm[i], gla_w_out[i], batch, seq)
        else:
            j = i - n_gla
            assert j == 0, "a single FoX layer is supported"
            q, sg, k_sh, v_sh, c_sh, ct_sh = _fox_proj(
                x2d, g_mix[i], kv_g_norm, fox_w_in[j], kv_w_in, kv_b_f, fox_g_qnorm[j],
                kv_g_knorm, batch, seq)
            x2d = _fox_attn(q, k_sh, v_sh, c_sh, ct_sh, sg, x2d, fox_w_out[j], batch, seq)
        x2d = _ffn_ple(x2d, p2d[i], g_ffn[i], ffn_w_up[i], ffn_w_dw[i], ffn_b_dw[i], ffn_w_down[i],
                       g_ple[i], ple_w_gate[i], ple_b_gate[i], ple_w_proj[i], g_final,
                       batch, seq, final_norm=(i == depth - 1))
    return x2d.reshape(batch, seq, d)
```

```python
import functools
import math

import jax
import jax.numpy as jnp
from jax import lax
from jax.experimental import pallas as pl
from jax.experimental.pallas import tpu as pltpu

F32 = jnp.float32
BF16 = jnp.bfloat16

EPS = 1e-6
D_MODEL = 1024
GLA_HEADS = 4
GLA_DK = 128
GLA_DV = 256
GLA_QK = GLA_HEADS * GLA_DK
GLA_V = GLA_HEADS * GLA_DV
GLA_RANK = 16
GLA_TAU = 16.0
GLA_CHUNK = 64
FOX_HEADS = 16
FOX_HD = 64
D_FF = 2816
PLE_DIM = 256
LOG2E = math.log2(math.e)

LANES = 128
SUBLANES = 8
VMEM_LIMIT = 56 * 1024 * 1024

TM_PROJ = 512
TM_GLA = 512
TM_FFN = 512
FFN_CHUNK = 256
FFN_UBUFS = 4
TQ = 256
TK = 256


def _dot(a, b):
    return jnp.dot(a, b, preferred_element_type=F32)


def _dot_nt(a, b):
    return lax.dot_general(a, b, (((1,), (1,)), ((), ())), preferred_element_type=F32)


def _dot_tn(a, b):
    return lax.dot_general(a, b, (((0,), (0,)), ((), ())), preferred_element_type=F32)


def _split3(x):
    hi = x.astype(BF16)
    r = x - hi.astype(F32)
    mid = r.astype(BF16)
    lo = (r - mid.astype(F32)).astype(BF16)
    return hi, mid, lo


def _rms(x, g):
    return x * lax.rsqrt(jnp.mean(x * x, axis=-1, keepdims=True) + EPS) * g


def _sigmoid(z):
    return 1.0 / (1.0 + jnp.exp(-z))


def _log_sigmoid(z):
    return jnp.minimum(z, 0.0) - jnp.log1p(jnp.exp(-jnp.abs(z)))


def _const_spec(shape):
    nd = len(shape)
    return pl.BlockSpec(shape, lambda *_: (0,) * nd, pipeline_mode=pl.Buffered(1))


def _params(sem):
    return pltpu.CompilerParams(dimension_semantics=sem, vmem_limit_bytes=VMEM_LIMIT)


def _gla_proj_kernel(x_ref, g_ref, w_ref, wg2_ref, bg2_ref, q_ref, k_ref, v_ref, og_ref, gk_ref):
    hn = _rms(x_ref[...], g_ref[...]).astype(BF16)
    q_ref[...] = _dot(hn, w_ref[:, 0:GLA_QK]).astype(BF16)
    k_ref[...] = _dot(hn, w_ref[:, GLA_QK:2 * GLA_QK]).astype(BF16)
    v_ref[...] = _dot(hn, w_ref[:, 2 * GLA_QK:2 * GLA_QK + GLA_V]).astype(BF16)
    og_ref[...] = _dot(hn, w_ref[:, 2 * GLA_QK + GLA_V:2 * GLA_QK + 2 * GLA_V]).astype(BF16)
    gr = _dot(hn, w_ref[:, 2 * GLA_QK + 2 * GLA_V:]).astype(BF16)
    z = _dot(gr, wg2_ref[...]) + bg2_ref[...]
    gk_ref[...] = _log_sigmoid(z) * (1.0 / GLA_TAU)


def _gla_proj(x2d, g, w_in, w_gk2, b_gk2):
    t = x2d.shape[0]
    tm = TM_PROJ
    n_main = 2 * GLA_QK + 2 * GLA_V
    w_pad = jnp.zeros((D_MODEL, n_main + LANES), BF16).at[:, :n_main + GLA_RANK].set(w_in.astype(BF16))
    wg2_pad = jnp.zeros((LANES, GLA_QK), BF16).at[:GLA_RANK].set(w_gk2.astype(BF16))
    row = lambda i: (i, 0)
    return pl.pallas_call(
        _gla_proj_kernel,
        grid=(t // tm,),
        in_specs=[
            pl.BlockSpec((tm, D_MODEL), row),
            _const_spec((1, D_MODEL)),
            _const_spec((D_MODEL, n_main + LANES)),
            _const_spec((LANES, GLA_QK)),
            _const_spec((1, GLA_QK)),
        ],
        out_specs=[
            pl.BlockSpec((tm, GLA_QK), row),
            pl.BlockSpec((tm, GLA_QK), row),
            pl.BlockSpec((tm, GLA_V), row),
            pl.BlockSpec((tm, GLA_V), row),
            pl.BlockSpec((tm, GLA_QK), row),
        ],
        out_shape=[
            jax.ShapeDtypeStruct((t, GLA_QK), BF16),
            jax.ShapeDtypeStruct((t, GLA_QK), BF16),
            jax.ShapeDtypeStruct((t, GLA_V), BF16),
            jax.ShapeDtypeStruct((t, GLA_V), BF16),
            jax.ShapeDtypeStruct((t, GLA_QK), F32),
        ],
        compiler_params=_params(("parallel",)),
        name="gla_proj",
    )(x2d, g.reshape(1, D_MODEL), w_pad, wg2_pad, b_gk2.reshape(1, GLA_QK))


def _gla_scan_kernel(q_ref, k_ref, v_ref, og_ref, gk_ref, gon_ref, out_ref, state_ref, o_ref):
    c = GLA_CHUNK
    tm = q_ref.shape[0]

    @pl.when(pl.program_id(1) == 0)
    def _():
        state_ref[...] = jnp.zeros_like(state_ref)

    rowi = lax.broadcasted_iota(jnp.int32, (c, c), 0)
    coli = lax.broadcasted_iota(jnp.int32, (c, c), 1)
    tril = rowi >= coli
    tril_bf = jnp.where(tril, 1.0, 0.0).astype(BF16)
    scale = GLA_DK ** -0.5

    def chunk(ci, carry):
        r0 = pl.multiple_of(ci * c, c)
        gk = gk_ref[pl.ds(r0, c), :]
        hi, mid, lo = _split3(gk)
        bcum = _dot(tril_bf, hi) + _dot(tril_bf, mid) + _dot(tril_bf, lo)
        b_last = bcum[c - 1:c, :]
        q = q_ref[pl.ds(r0, c), :].astype(F32) * scale
        k = k_ref[pl.ds(r0, c), :].astype(F32)
        q_in = (q * jnp.exp(bcum)).astype(BF16)
        k_in = (k * jnp.exp(-bcum)).astype(BF16)
        k_end = (k * jnp.exp(b_last - bcum)).astype(BF16)
        decay = jnp.exp(b_last)
        for h in range(GLA_HEADS):
            ks = slice(h * GLA_DK, (h + 1) * GLA_DK)
            vs = slice(h * GLA_DV, (h + 1) * GLA_DV)
            att = _dot_nt(q_in[:, ks], k_in[:, ks])
            att = jnp.where(tril, att, 0.0).astype(BF16)
            v_h = v_ref[pl.ds(r0, c), vs]
            st = state_ref[h]
            o_ref[pl.ds(r0, c), vs] = _dot(att, v_h) + _dot(q_in[:, ks], st.astype(BF16))
            dcol = jnp.broadcast_to(decay[:, ks], (LANES, GLA_DK)).T[:, 0:1]
            state_ref[h] = dcol * st + _dot_tn(k_end[:, ks], v_h)
        return carry

    lax.fori_loop(0, tm // c, chunk, 0)

    for h in range(GLA_HEADS):
        vs = slice(h * GLA_DV, (h + 1) * GLA_DV)
        on = _rms(o_ref[:, vs], gon_ref[...])
        og = og_ref[:, vs].astype(F32)
        out_ref[:, vs] = (on * (og * _sigmoid(og))).astype(BF16)


def _gla_scan(q, k, v, og, gk, g_onorm, batch, seq):
    tm = TM_GLA
    nt = seq // tm
    row = lambda b, s: (b * nt + s, 0)
    return pl.pallas_call(
        _gla_scan_kernel,
        grid=(batch, nt),
        in_specs=[
            pl.BlockSpec((tm, GLA_QK), row),
            pl.BlockSpec((tm, GLA_QK), row),
            pl.BlockSpec((tm, GLA_V), row),
            pl.BlockSpec((tm, GLA_V), row),
            pl.BlockSpec((tm, GLA_QK), row),
            _const_spec((1, GLA_DV)),
        ],
        out_specs=pl.BlockSpec((tm, GLA_V), row),
        out_shape=jax.ShapeDtypeStruct((batch * seq, GLA_V), BF16),
        scratch_shapes=[
            pltpu.VMEM((GLA_HEADS, GLA_DK, GLA_DV), F32),
            pltpu.VMEM((tm, GLA_V), F32),
        ],
        compiler_params=_params(("parallel", "arbitrary")),
        name="gla_scan",
    )(q, k, v, og, gk, g_onorm.reshape(1, GLA_DV))


def _ffn_kernel(x_ref, mix_ref, wmix_ref, p_ref, gf_ref, wup_ref, wdw_ref, bdw_ref, wdn_ref, gp_ref,
                wg_ref, bg_ref, wp_ref, gfin_ref, out_ref, carry_ref, ubuf_ref, act_ref, *, final_norm):
    tm = x_ref.shape[0]
    fc = FFN_CHUNK
    nch = D_FF // fc
    hdr = SUBLANES

    @pl.when(pl.program_id(1) == 0)
    def _():
        carry_ref[...] = jnp.zeros_like(carry_ref)

    x = x_ref[...] + _dot(mix_ref[...], wmix_ref[...])
    hn = _rms(x, gf_ref[...]).astype(BF16)

    def conv(col0, slot):
        u = _dot(hn, wup_ref[:, col0:col0 + fc])
        buf = ubuf_ref.at[slot % FFN_UBUFS]
        buf[0:hdr, :] = carry_ref[slot]
        buf[hdr:hdr + tm, :] = u
        carry_ref[slot] = u[tm - hdr:tm, :]
        u1 = buf[hdr - 1:hdr - 1 + tm, :]
        u2 = buf[hdr - 2:hdr - 2 + tm, :]
        w = wdw_ref[:, col0:col0 + fc]
        return w[0:1] * u2 + w[1:2] * u1 + w[2:3] * u + bdw_ref[:, col0:col0 + fc]

    for ci in range(nch):
        a = conv(ci * fc, 2 * ci)
        b = conv(D_FF + ci * fc, 2 * ci + 1)
        act_ref[:, ci * fc:(ci + 1) * fc] = (a * _sigmoid(a) * b).astype(BF16)

    y = x + _dot(act_ref[...], wdn_ref[...])
    hp = _rms(y, gp_ref[...]).astype(BF16)
    gate = _sigmoid(_dot(hp, wg_ref[...]) + bg_ref[...])
    pe = _dot(p_ref[...].astype(BF16), wp_ref[...])
    y = y + pe * gate
    if final_norm:
        y = _rms(y, gfin_ref[...])
    out_ref[...] = y


def _ffn_ple(x2d, mix, w_mix, p2d, g_ffn, w_up, w_dw, b_dw, w_down, g_ple, w_gate, b_gate, w_proj,
             g_final, batch, seq, final_norm):
    tm = TM_FFN
    nt = seq // tm
    nslots = 2 * (D_FF // FFN_CHUNK)
    row = lambda b, s: (b * nt + s, 0)
    return pl.pallas_call(
        functools.partial(_ffn_kernel, final_norm=final_norm),
        grid=(batch, nt),
        in_specs=[
            pl.BlockSpec((tm, D_MODEL), row),
            pl.BlockSpec((tm, D_MODEL), row),
            _const_spec((D_MODEL, D_MODEL)),
            pl.BlockSpec((tm, PLE_DIM), row),
            _const_spec((1, D_MODEL)),
            _const_spec((D_MODEL, 2 * D_FF)),
            _const_spec((3, 2 * D_FF)),
            _const_spec((1, 2 * D_FF)),
            _const_spec((D_FF, D_MODEL)),
            _const_spec((1, D_MODEL)),
            _const_spec((D_MODEL, D_MODEL)),
            _const_spec((1, D_MODEL)),
            _const_spec((PLE_DIM, D_MODEL)),
            _const_spec((1, D_MODEL)),
        ],
        out_specs=pl.BlockSpec((tm, D_MODEL), row),
        out_shape=jax.ShapeDtypeStruct(x2d.shape, F32),
        scratch_shapes=[
            pltpu.VMEM((nslots, SUBLANES, FFN_CHUNK), F32),
            pltpu.VMEM((FFN_UBUFS, SUBLANES + tm, FFN_CHUNK), F32),
            pltpu.VMEM((tm, D_FF), BF16),
        ],
        compiler_params=_params(("parallel", "arbitrary")),
        name="ffn_ple",
    )(x2d, mix, w_mix.astype(BF16), p2d, g_ffn.reshape(1, -1), w_up.astype(BF16), w_dw,
      b_dw.reshape(1, -1), w_down.astype(BF16), g_ple.reshape(1, -1), w_gate.astype(BF16),
      b_gate.reshape(1, -1), w_proj.astype(BF16), g_final.reshape(1, -1))


def _fox_proj_kernel(x_ref, gm_ref, gkv_ref, wq_ref, wog_ref, wk_ref, wv_ref, wf_ref, wft_ref,
                     bf_ref, bfc_ref, gq_ref, gkn_ref, ones_ref, tril_ref, triu_ref,
                     q_ref, sg_ref, k_ref, v_ref, c_ref, ct_ref, crow_ref, ccol_ref):
    tm = x_ref.shape[0]

    @pl.when(pl.program_id(1) == 0)
    def _():
        crow_ref[...] = jnp.zeros_like(crow_ref)
        ccol_ref[...] = jnp.zeros_like(ccol_ref)

    x = x_ref[...]
    xn = x * lax.rsqrt(jnp.mean(x * x, axis=-1, keepdims=True) + EPS)
    h1 = (xn * gm_ref[...]).astype(BF16)
    h2 = (xn * gkv_ref[...]).astype(BF16)

    def head_norm(t, g):
        t2 = (t * t).astype(BF16)
        blk = 2 * LANES
        ss = jnp.concatenate(
            [_dot(t2[:, i * blk:(i + 1) * blk], ones_ref[...]) for i in range(D_MODEL // blk)], axis=1)
        return t * lax.rsqrt(ss * (1.0 / FOX_HD) + EPS) * g

    q = _dot(h1, wq_ref[...])
    q_ref[...] = (head_norm(q, gq_ref[...]) * (FOX_HD ** -0.5 * LOG2E)).astype(BF16)
    sg_ref[...] = _sigmoid(_dot(h1, wog_ref[...])).astype(BF16)
    k = _dot(h2, wk_ref[...])
    k_ref[...] = head_norm(k, gkn_ref[...]).astype(BF16)
    v_ref[...] = _dot(h2, wv_ref[...]).astype(BF16)

    lf = _log_sigmoid(_dot(h2, wf_ref[...]) + bf_ref[...])
    hi, mid, lo = _split3(lf)
    tril = tril_ref[...]
    c = _dot(tril, hi) + _dot(tril, mid) + _dot(tril, lo) + crow_ref[0:1, :]
    c_ref[...] = c * LOG2E
    crow_ref[...] = jnp.broadcast_to(c[tm - 1:tm, :], crow_ref.shape)

    lft = _log_sigmoid(_dot_nt(wft_ref[...], h2) + bfc_ref[:, 0:1])
    hi, mid, lo = _split3(lft)
    triu = triu_ref[...]
    ct = _dot(hi, triu) + _dot(mid, triu) + _dot(lo, triu) + ccol_ref[:, 0:1]
    ct2 = ct * LOG2E
    for j in range(FOX_HEADS // 2):
        ct_ref[0, j] = ct2[2 * j:2 * j + 2, :]
    ccol_ref[...] = jnp.broadcast_to(ct[:, tm - 1:tm], ccol_ref.shape)


def _fox_proj(x2d, g_mix, g_kv, w_fox_in, w_kv_in, b_f, g_qnorm, g_knorm, batch, seq):
    tm = TM_PROJ
    nt = seq // tm
    d = D_MODEL
    wq = w_fox_in[:, :d].astype(BF16)
    wog = w_fox_in[:, d:].astype(BF16)
    wk = w_kv_in[:, :d].astype(BF16)
    wv = w_kv_in[:, d:2 * d].astype(BF16)
    wf = w_kv_in[:, 2 * d:]
    wf_pad = jnp.zeros((d, LANES), BF16).at[:, :FOX_HEADS].set(wf.astype(BF16))
    wft = wf.T.astype(BF16)
    bf_pad = jnp.zeros((1, LANES), F32).at[0, :FOX_HEADS].set(b_f)
    bf_col = jnp.broadcast_to(b_f[:, None], (FOX_HEADS, LANES))
    gq = jnp.tile(g_qnorm, FOX_HEADS).reshape(1, d)
    gkn = jnp.tile(g_knorm, FOX_HEADS).reshape(1, d)
    blk = 2 * LANES
    hid = jnp.arange(blk) // FOX_HD
    ones_bd = (hid[:, None] == hid[None, :]).astype(BF16)
    tid = jnp.arange(tm)
    tril = (tid[:, None] >= tid[None, :]).astype(BF16)
    triu = tril.T
    row = lambda b, s: (b * nt + s, 0)
    return pl.pallas_call(
        _fox_proj_kernel,
        grid=(batch, nt),
        in_specs=[
            pl.BlockSpec((tm, d), row),
            _const_spec((1, d)), _const_spec((1, d)),
            _const_spec((d, d)), _const_spec((d, d)), _const_spec((d, d)), _const_spec((d, d)),
            _const_spec((d, LANES)), _const_spec((FOX_HEADS, d)),
            _const_spec((1, LANES)), _const_spec((FOX_HEADS, LANES)),
            _const_spec((1, d)), _const_spec((1, d)),
            _const_spec((blk, blk)), _const_spec((tm, tm)), _const_spec((tm, tm)),
        ],
        out_specs=[
            pl.BlockSpec((tm, d), row),
            pl.BlockSpec((tm, d), row),
            pl.BlockSpec((tm, d), row),
            pl.BlockSpec((tm, d), row),
            pl.BlockSpec((tm, LANES), row),
            pl.BlockSpec((1, FOX_HEADS // 2, 2, tm), lambda b, s: (b, 0, 0, s)),
        ],
        out_shape=[
            jax.ShapeDtypeStruct((batch * seq, d), BF16),
            jax.ShapeDtypeStruct((batch * seq, d), BF16),
            jax.ShapeDtypeStruct((batch * seq, d), BF16),
            jax.ShapeDtypeStruct((batch * seq, d), BF16),
            jax.ShapeDtypeStruct((batch * seq, LANES), F32),
            jax.ShapeDtypeStruct((batch, FOX_HEADS // 2, 2, seq), F32),
        ],
        scratch_shapes=[pltpu.VMEM((SUBLANES, LANES), F32), pltpu.VMEM((FOX_HEADS, LANES), F32)],
        compiler_params=_params(("parallel", "arbitrary")),
        name="fox_proj",
    )(x2d, g_mix.reshape(1, d), g_kv.reshape(1, d), wq, wog, wk, wv, wf_pad, wft, bf_pad, bf_col,
      gq, gkn, ones_bd, tril, triu)


def _fox_attn_kernel(q_ref, k_ref, v_ref, c_ref, ct_ref, sg_ref, out_ref, s_ref):
    seq = q_ref.shape[0]
    pair = pl.program_id(1)
    lane = lax.broadcasted_iota(jnp.int32, (TQ, LANES), 1)
    low = lane < FOX_HD
    causal = (lax.broadcasted_iota(jnp.int32, (TQ, TK), 0)
              >= lax.broadcasted_iota(jnp.int32, (TQ, TK), 1))

    for qi in range(seq // TQ):
        rows = slice(qi * TQ, (qi + 1) * TQ)
        qp = q_ref[rows, :]
        cblk = c_ref[rows, :]
        zero = jnp.zeros_like(qp)
        qs = jnp.concatenate([jnp.where(low, qp, zero), jnp.where(low, zero, qp)], axis=0)
        cq = [jnp.sum(jnp.where(lane == 2 * pair + hh, cblk, 0.0), axis=1, keepdims=True)
              for hh in range(2)]
        sbuf = s_ref.at[qi % 2]
        mrun = None
        for kb in range(qi + 1):
            cols = slice(kb * TK, (kb + 1) * TK)
            sb = _dot_nt(qs, k_ref[cols, :])
            halves = []
            for hh in range(2):
                s = sb[hh * TQ:(hh + 1) * TQ] + (cq[hh] - ct_ref[0, 0, hh:hh + 1, cols])
                if kb == qi:
                    s = jnp.where(causal, s, -jnp.inf)
                halves.append(s)
            s = jnp.concatenate(halves, axis=0)
            sbuf[:, cols] = s
            bm = jnp.maximum(s[:, :LANES], s[:, LANES:])
            mrun = bm if mrun is None else jnp.maximum(mrun, bm)
        m = jnp.max(mrun, axis=1, keepdims=True)
        acc = None
        lrun = None
        for kb in range(qi + 1):
            cols = slice(kb * TK, (kb + 1) * TK)
            p = jnp.exp2(sbuf[:, cols] - m)
            ps = p[:, :LANES] + p[:, LANES:]
            lrun = ps if lrun is None else lrun + ps
            d = _dot(p.astype(BF16), v_ref[cols, :])
            acc = d if acc is None else acc + d
        on = acc / jnp.sum(lrun, axis=1, keepdims=True)
        o = jnp.where(low, on[:TQ], on[TQ:])
        out_ref[rows, :] = (o * sg_ref[rows, :].astype(F32)).astype(BF16)


def _fox_attn(q, k, v, c, ct, sg, batch, seq):
    d = D_MODEL
    pairs = FOX_HEADS // 2
    blk = lambda b, j: (b, j)
    return pl.pallas_call(
        _fox_attn_kernel,
        grid=(batch, pairs),
        in_specs=[
            pl.BlockSpec((seq, LANES), blk),
            pl.BlockSpec((seq, LANES), blk),
            pl.BlockSpec((seq, LANES), blk),
            pl.BlockSpec((seq, LANES), lambda b, j: (b, 0)),
            pl.BlockSpec((1, 1, 2, seq), lambda b, j: (b, j, 0, 0)),
            pl.BlockSpec((seq, LANES), blk),
        ],
        out_specs=pl.BlockSpec((seq, LANES), blk),
        out_shape=jax.ShapeDtypeStruct((batch * seq, d), BF16),
        scratch_shapes=[pltpu.VMEM((2, 2 * TQ, seq), F32)],
        compiler_params=_params(("parallel", "parallel")),
        name="fox_attn",
    )(q, k, v, c, ct, sg)


def kernel(x, p, g_mix, g_ffn, g_ple, gla_w_in, gla_w_gk2, gla_b_gk2, gla_g_onorm, gla_w_out,
           kv_g_norm, kv_w_in, kv_g_knorm, kv_b_f, fox_w_in, fox_g_qnorm, fox_w_out, ffn_w_up,
           ffn_w_dw, ffn_b_dw, ffn_w_down, ple_w_gate, ple_b_gate, ple_w_proj, g_final):
    batch, seq, d = x.shape
    depth = p.shape[0]
    n_gla = gla_w_in.shape[0]
    x2d = x.reshape(batch * seq, d)
    p2d = p.reshape(depth, batch * seq, PLE_DIM)
    for i in range(depth):
        if i < n_gla:
            q, k, v, og, gk = _gla_proj(x2d, g_mix[i], gla_w_in[i], gla_w_gk2[i], gla_b_gk2[i])
            mix = _gla_scan(q, k, v, og, gk, gla_g_onorm[i], batch, seq)
            w_mix = gla_w_out[i]
        else:
            j = i - n_gla
            assert j == 0, "a single FoX layer is supported"
            q, sg, k_sh, v_sh, c_sh, ct_sh = _fox_proj(
                x2d, g_mix[i], kv_g_norm, fox_w_in[j], kv_w_in, kv_b_f, fox_g_qnorm[j],
                kv_g_knorm, batch, seq)
            mix = _fox_attn(q, k_sh, v_sh, c_sh, ct_sh, sg, batch, seq)
            w_mix = fox_w_out[j]
        x2d = _ffn_ple(x2d, mix, w_mix, p2d[i], g_ffn[i], ffn_w_up[i], ffn_w_dw[i], ffn_b_dw[i],
                       ffn_w_down[i], g_ple[i], ple_w_gate[i], ple_b_gate[i], ple_w_proj[i],
                       g_final, batch, seq, final_norm=(i == depth - 1))
    return x2d.reshape(batch, seq, d)
```

```python
import functools
import math

import jax
import jax.numpy as jnp
from jax import lax
from jax.experimental import pallas as pl
from jax.experimental.pallas import tpu as pltpu

F32 = jnp.float32
BF16 = jnp.bfloat16

EPS = 1e-6
D_MODEL = 1024
GLA_HEADS = 4
GLA_DK = 128
GLA_DV = 256
GLA_QK = GLA_HEADS * GLA_DK
GLA_V = GLA_HEADS * GLA_DV
GLA_RANK = 16
GLA_TAU = 16.0
GLA_CHUNK = 64
FOX_HEADS = 16
FOX_HD = 64
D_FF = 2816
PLE_DIM = 256
LOG2E = math.log2(math.e)

LANES = 128
SUBLANES = 8
VMEM_LIMIT = 56 * 1024 * 1024

TM_PROJ = 512
TM_GLA = 512
TM_FFN = 512
FFN_CHUNK = 256
FFN_SPLIT = 1
TQ = 256
TK = 256


def _dot(a, b):
    return jnp.dot(a, b, preferred_element_type=F32)


def _dot_nt(a, b):
    return lax.dot_general(a, b, (((1,), (1,)), ((), ())), preferred_element_type=F32)


def _dot_tn(a, b):
    return lax.dot_general(a, b, (((0,), (0,)), ((), ())), preferred_element_type=F32)


def _split3(x):
    hi = x.astype(BF16)
    r = x - hi.astype(F32)
    mid = r.astype(BF16)
    lo = (r - mid.astype(F32)).astype(BF16)
    return hi, mid, lo


def _rms(x, g):
    return x * lax.rsqrt(jnp.mean(x * x, axis=-1, keepdims=True) + EPS) * g


def _sigmoid(z):
    return 1.0 / (1.0 + jnp.exp(-z))


def _log_sigmoid(z):
    return jnp.minimum(z, 0.0) - jnp.log1p(jnp.exp(-jnp.abs(z)))


def _const_spec(shape):
    nd = len(shape)
    return pl.BlockSpec(shape, lambda *_: (0,) * nd, pipeline_mode=pl.Buffered(1))


def _params(sem):
    return pltpu.CompilerParams(dimension_semantics=sem, vmem_limit_bytes=VMEM_LIMIT)


def _gla_proj_kernel(x_ref, g_ref, w_ref, wg2_ref, bg2_ref, q_ref, k_ref, v_ref, og_ref, gk_ref):
    hn = _rms(x_ref[...], g_ref[...]).astype(BF16)
    q_ref[...] = _dot(hn, w_ref[:, 0:GLA_QK]).astype(BF16)
    k_ref[...] = _dot(hn, w_ref[:, GLA_QK:2 * GLA_QK]).astype(BF16)
    v_ref[...] = _dot(hn, w_ref[:, 2 * GLA_QK:2 * GLA_QK + GLA_V]).astype(BF16)
    og_ref[...] = _dot(hn, w_ref[:, 2 * GLA_QK + GLA_V:2 * GLA_QK + 2 * GLA_V]).astype(BF16)
    gr = _dot(hn, w_ref[:, 2 * GLA_QK + 2 * GLA_V:]).astype(BF16)
    z = _dot(gr, wg2_ref[...]) + bg2_ref[...]
    gk_ref[...] = _log_sigmoid(z) * (1.0 / GLA_TAU)


def _gla_proj(x2d, g, w_in, w_gk2, b_gk2):
    t = x2d.shape[0]
    tm = TM_PROJ
    n_main = 2 * GLA_QK + 2 * GLA_V
    w_pad = jnp.zeros((D_MODEL, n_main + LANES), BF16).at[:, :n_main + GLA_RANK].set(w_in.astype(BF16))
    wg2_pad = jnp.zeros((LANES, GLA_QK), BF16).at[:GLA_RANK].set(w_gk2.astype(BF16))
    row = lambda i: (i, 0)
    return pl.pallas_call(
        _gla_proj_kernel,
        grid=(t // tm,),
        in_specs=[
            pl.BlockSpec((tm, D_MODEL), row),
            _const_spec((1, D_MODEL)),
            _const_spec((D_MODEL, n_main + LANES)),
            _const_spec((LANES, GLA_QK)),
            _const_spec((1, GLA_QK)),
        ],
        out_specs=[
            pl.BlockSpec((tm, GLA_QK), row),
            pl.BlockSpec((tm, GLA_QK), row),
            pl.BlockSpec((tm, GLA_V), row),
            pl.BlockSpec((tm, GLA_V), row),
            pl.BlockSpec((tm, GLA_QK), row),
        ],
        out_shape=[
            jax.ShapeDtypeStruct((t, GLA_QK), BF16),
            jax.ShapeDtypeStruct((t, GLA_QK), BF16),
            jax.ShapeDtypeStruct((t, GLA_V), BF16),
            jax.ShapeDtypeStruct((t, GLA_V), BF16),
            jax.ShapeDtypeStruct((t, GLA_QK), F32),
        ],
        compiler_params=_params(("parallel",)),
        name="gla_proj",
    )(x2d, g.reshape(1, D_MODEL), w_pad, wg2_pad, b_gk2.reshape(1, GLA_QK))


def _gla_scan_kernel(q_ref, k_ref, v_ref, og_ref, gk_ref, gon_ref, out_ref, state_ref, o_ref):
    c = GLA_CHUNK
    tm = q_ref.shape[0]

    @pl.when(pl.program_id(1) == 0)
    def _():
        state_ref[...] = jnp.zeros_like(state_ref)

    rowi = lax.broadcasted_iota(jnp.int32, (c, c), 0)
    coli = lax.broadcasted_iota(jnp.int32, (c, c), 1)
    tril = rowi >= coli
    tril_bf = jnp.where(tril, 1.0, 0.0).astype(BF16)
    scale = GLA_DK ** -0.5

    nchunk = tm // c
    chunk_rows = [slice(ci * c, (ci + 1) * c) for ci in range(nchunk)]
    head_k = [slice(h * GLA_DK, (h + 1) * GLA_DK) for h in range(GLA_HEADS)]
    head_v = [slice(h * GLA_DV, (h + 1) * GLA_DV) for h in range(GLA_HEADS)]

    hi, mid, lo = _split3(gk_ref[...])
    bcum = jnp.concatenate(
        [_dot(tril_bf, hi[r]) + _dot(tril_bf, mid[r]) + _dot(tril_bf, lo[r]) for r in chunk_rows],
        axis=0)
    b_last = [bcum[r.stop - 1:r.stop, :] for r in chunk_rows]
    b_last_rows = jnp.concatenate([jnp.broadcast_to(b, (c, GLA_QK)) for b in b_last], axis=0)
    q = q_ref[...].astype(F32) * scale
    k = k_ref[...].astype(F32)
    q_in = (q * jnp.exp(bcum)).astype(BF16)
    k_in = (k * jnp.exp(-bcum)).astype(BF16)
    k_end = (k * jnp.exp(b_last_rows - bcum)).astype(BF16)
    decay = jnp.exp(jnp.concatenate(b_last, axis=0))
    pad = jnp.zeros((LANES - nchunk, GLA_DK), F32)
    decay_t = [jnp.concatenate([decay[:, ks], pad], axis=0).T for ks in head_k]

    att = [[jnp.where(tril, _dot_nt(q_in[r, ks], k_in[r, ks]), 0.0).astype(BF16) for ks in head_k]
           for r in chunk_rows]
    upd = [[_dot_tn(k_end[r, head_k[h]], v_ref[r, head_v[h]]) for h in range(GLA_HEADS)]
           for r in chunk_rows]

    state = [state_ref[h] for h in range(GLA_HEADS)]
    for ci, r in enumerate(chunk_rows):
        for h in range(GLA_HEADS):
            o_ref[r, head_v[h]] = (_dot(att[ci][h], v_ref[r, head_v[h]])
                                   + _dot(q_in[r, head_k[h]], state[h].astype(BF16)))
            state[h] = decay_t[h][:, ci:ci + 1] * state[h] + upd[ci][h]
    for h in range(GLA_HEADS):
        state_ref[h] = state[h]

    for h in range(GLA_HEADS):
        vs = slice(h * GLA_DV, (h + 1) * GLA_DV)
        on = _rms(o_ref[:, vs], gon_ref[...])
        og = og_ref[:, vs].astype(F32)
        out_ref[:, vs] = (on * (og * _sigmoid(og))).astype(BF16)


def _gla_scan(q, k, v, og, gk, g_onorm, batch, seq):
    tm = TM_GLA
    nt = seq // tm
    row = lambda b, s: (b * nt + s, 0)
    return pl.pallas_call(
        _gla_scan_kernel,
        grid=(batch, nt),
        in_specs=[
            pl.BlockSpec((tm, GLA_QK), row),
            pl.BlockSpec((tm, GLA_QK), row),
            pl.BlockSpec((tm, GLA_V), row),
            pl.BlockSpec((tm, GLA_V), row),
            pl.BlockSpec((tm, GLA_QK), row),
            _const_spec((1, GLA_DV)),
        ],
        out_specs=pl.BlockSpec((tm, GLA_V), row),
        out_shape=jax.ShapeDtypeStruct((batch * seq, GLA_V), BF16),
        scratch_shapes=[
            pltpu.VMEM((GLA_HEADS, GLA_DK, GLA_DV), F32),
            pltpu.VMEM((tm, GLA_V), F32),
        ],
        compiler_params=_params(("parallel", "arbitrary")),
        name="gla_scan",
    )(q, k, v, og, gk, g_onorm.reshape(1, GLA_DV))


def _ffn_kernel(x_ref, mix_ref, wmix_ref, p_ref, gf_ref, wup_ref, wdw_ref, bdw_ref, wdn_ref, gp_ref,
                wg_ref, bg_ref, wp_ref, gfin_ref, out_ref, carry_ref, act_ref, *, final_norm):
    tm = x_ref.shape[0]
    fc = FFN_CHUNK
    nch = D_FF // fc
    hdr = SUBLANES
    rows = tm // FFN_SPLIT
    blocks = [slice(s * rows, (s + 1) * rows) for s in range(FFN_SPLIT)]

    @pl.when(pl.program_id(1) == 0)
    def _():
        carry_ref[...] = jnp.zeros_like(carry_ref)

    xs = [x_ref[r, :] + _dot(mix_ref[r, :], wmix_ref[...]) for r in blocks]
    hns = [_rms(x, gf_ref[...]).astype(BF16) for x in xs]

    def conv(hn, col0, above):
        u = _dot(hn, wup_ref[:, col0:col0 + fc])
        ext = jnp.concatenate([above, u], axis=0)
        u1 = ext[hdr - 1:hdr - 1 + rows, :]
        u2 = ext[hdr - 2:hdr - 2 + rows, :]
        w = wdw_ref[:, col0:col0 + fc]
        return (w[0:1] * u2 + w[1:2] * u1 + w[2:3] * u + bdw_ref[:, col0:col0 + fc],
                u[rows - hdr:rows, :])

    for ci in range(nch):
        above_a = carry_ref[2 * ci]
        above_b = carry_ref[2 * ci + 1]
        for s, r in enumerate(blocks):
            a, above_a = conv(hns[s], ci * fc, above_a)
            b, above_b = conv(hns[s], D_FF + ci * fc, above_b)
            act_ref[r, ci * fc:(ci + 1) * fc] = (a * _sigmoid(a) * b).astype(BF16)
        carry_ref[2 * ci] = above_a
        carry_ref[2 * ci + 1] = above_b

    for s, r in enumerate(blocks):
        y = xs[s] + _dot(act_ref[r, :], wdn_ref[...])
        hp = _rms(y, gp_ref[...]).astype(BF16)
        gate = _sigmoid(_dot(hp, wg_ref[...]) + bg_ref[...])
        pe = _dot(p_ref[r, :].astype(BF16), wp_ref[...])
        y = y + pe * gate
        if final_norm:
            y = _rms(y, gfin_ref[...])
        out_ref[r, :] = y


def _ffn_ple(x2d, mix, w_mix, p2d, g_ffn, w_up, w_dw, b_dw, w_down, g_ple, w_gate, b_gate, w_proj,
             g_final, batch, seq, final_norm):
    tm = TM_FFN
    nt = seq // tm
    nslots = 2 * (D_FF // FFN_CHUNK)
    row = lambda b, s: (b * nt + s, 0)
    return pl.pallas_call(
        functools.partial(_ffn_kernel, final_norm=final_norm),
        grid=(batch, nt),
        in_specs=[
            pl.BlockSpec((tm, D_MODEL), row),
            pl.BlockSpec((tm, D_MODEL), row),
            _const_spec((D_MODEL, D_MODEL)),
            pl.BlockSpec((tm, PLE_DIM), row),
            _const_spec((1, D_MODEL)),
            _const_spec((D_MODEL, 2 * D_FF)),
            _const_spec((3, 2 * D_FF)),
            _const_spec((1, 2 * D_FF)),
            _const_spec((D_FF, D_MODEL)),
            _const_spec((1, D_MODEL)),
            _const_spec((D_MODEL, D_MODEL)),
            _const_spec((1, D_MODEL)),
            _const_spec((PLE_DIM, D_MODEL)),
            _const_spec((1, D_MODEL)),
        ],
        out_specs=pl.BlockSpec((tm, D_MODEL), row),
        out_shape=jax.ShapeDtypeStruct(x2d.shape, F32),
        scratch_shapes=[
            pltpu.VMEM((nslots, SUBLANES, FFN_CHUNK), F32),
            pltpu.VMEM((tm, D_FF), BF16),
        ],
        compiler_params=_params(("parallel", "arbitrary")),
        name="ffn_ple",
    )(x2d, mix, w_mix.astype(BF16), p2d, g_ffn.reshape(1, -1), w_up.astype(BF16), w_dw,
      b_dw.reshape(1, -1), w_down.astype(BF16), g_ple.reshape(1, -1), w_gate.astype(BF16),
      b_gate.reshape(1, -1), w_proj.astype(BF16), g_final.reshape(1, -1))


def _fox_proj_kernel(x_ref, gm_ref, gkv_ref, wq_ref, wog_ref, wk_ref, wvt_ref, wf_ref, bf_ref,
                     gq_ref, gkn_ref, ones_ref, tril_ref,
                     q_ref, sg_ref, k_ref, vt_ref, cs_ref, ct_ref, crow_ref):
    tm = x_ref.shape[0]

    @pl.when(pl.program_id(1) == 0)
    def _():
        crow_ref[...] = jnp.zeros_like(crow_ref)

    x = x_ref[...]
    xn = x * lax.rsqrt(jnp.mean(x * x, axis=-1, keepdims=True) + EPS)
    h1 = (xn * gm_ref[...]).astype(BF16)
    h2 = (xn * gkv_ref[...]).astype(BF16)

    def head_norm(t, g):
        t2 = (t * t).astype(BF16)
        blk = 2 * LANES
        ss = jnp.concatenate(
            [_dot(t2[:, i * blk:(i + 1) * blk], ones_ref[...]) for i in range(D_MODEL // blk)], axis=1)
        return t * lax.rsqrt(ss * (1.0 / FOX_HD) + EPS) * g

    q = _dot(h1, wq_ref[...])
    q_ref[...] = (head_norm(q, gq_ref[...]) * (FOX_HD ** -0.5 * LOG2E)).astype(BF16)
    sg_ref[...] = _sigmoid(_dot(h1, wog_ref[...])).astype(BF16)
    k = _dot(h2, wk_ref[...])
    k_ref[...] = head_norm(k, gkn_ref[...]).astype(BF16)
    vt_ref[...] = _dot_nt(wvt_ref[...], h2).astype(BF16)

    lf = _log_sigmoid(_dot(h2, wf_ref[...]) + bf_ref[...])
    hi, mid, lo = _split3(lf)
    tril = tril_ref[...]
    c = _dot(tril, hi) + _dot(tril, mid) + _dot(tril, lo) + crow_ref[0:1, :]
    crow_ref[...] = jnp.broadcast_to(c[tm - 1:tm, :], crow_ref.shape)
    c2 = c * LOG2E
    ct_ref[...] = c2.T[0:FOX_HEADS, :]
    hi = c2.astype(BF16).astype(F32)
    r = c2 - hi
    mid = r.astype(BF16).astype(F32)
    lo = r - mid
    lane = lax.broadcasted_iota(jnp.int32, c2.shape, 1)
    cs_ref[...] = jnp.where(
        lane < FOX_HEADS, hi,
        jnp.where(lane < 2 * FOX_HEADS, pltpu.roll(mid, FOX_HEADS, 1),
                  pltpu.roll(lo, 2 * FOX_HEADS, 1))).astype(BF16)


def _fox_proj(x2d, g_mix, g_kv, w_fox_in, w_kv_in, b_f, g_qnorm, g_knorm, batch, seq):
    tm = TM_PROJ
    nt = seq // tm
    d = D_MODEL
    wq = w_fox_in[:, :d].astype(BF16)
    wog = w_fox_in[:, d:].astype(BF16)
    wk = w_kv_in[:, :d].astype(BF16)
    wvt = w_kv_in[:, d:2 * d].T.astype(BF16)
    wf_pad = jnp.zeros((d, LANES), BF16).at[:, :FOX_HEADS].set(w_kv_in[:, 2 * d:].astype(BF16))
    bf_pad = jnp.zeros((1, LANES), F32).at[0, :FOX_HEADS].set(b_f)
    gq = jnp.tile(g_qnorm, FOX_HEADS).reshape(1, d)
    gkn = jnp.tile(g_knorm, FOX_HEADS).reshape(1, d)
    blk = 2 * LANES
    hid = jnp.arange(blk) // FOX_HD
    ones_bd = (hid[:, None] == hid[None, :]).astype(BF16)
    tid = jnp.arange(tm)
    tril = (tid[:, None] >= tid[None, :]).astype(BF16)
    row = lambda b, s: (b * nt + s, 0)
    return pl.pallas_call(
        _fox_proj_kernel,
        grid=(batch, nt),
        in_specs=[
            pl.BlockSpec((tm, d), row),
            _const_spec((1, d)), _const_spec((1, d)),
            _const_spec((d, d)), _const_spec((d, d)), _const_spec((d, d)), _const_spec((d, d)),
            _const_spec((d, LANES)), _const_spec((1, LANES)),
            _const_spec((1, d)), _const_spec((1, d)),
            _const_spec((blk, blk)), _const_spec((tm, tm)),
        ],
        out_specs=[
            pl.BlockSpec((tm, d), row),
            pl.BlockSpec((tm, d), row),
            pl.BlockSpec((tm, d), row),
            pl.BlockSpec((d, tm), lambda b, s: (b, s)),
            pl.BlockSpec((tm, LANES), row),
            pl.BlockSpec((FOX_HEADS, tm), lambda b, s: (b, s)),
        ],
        out_shape=[
            jax.ShapeDtypeStruct((batch * seq, d), BF16),
            jax.ShapeDtypeStruct((batch * seq, d), BF16),
            jax.ShapeDtypeStruct((batch * seq, d), BF16),
            jax.ShapeDtypeStruct((batch * d, seq), BF16),
            jax.ShapeDtypeStruct((batch * seq, LANES), BF16),
            jax.ShapeDtypeStruct((batch * FOX_HEADS, seq), F32),
        ],
        scratch_shapes=[pltpu.VMEM((SUBLANES, LANES), F32)],
        compiler_params=_params(("parallel", "arbitrary")),
        name="fox_proj",
    )(x2d, g_mix.reshape(1, d), g_kv.reshape(1, d), wq, wog, wk, wvt, wf_pad, bf_pad, gq, gkn,
      ones_bd, tril)


def _fox_attn_kernel(q_ref, k_ref, vt_ref, cs_ref, ct_ref, sg_ref, out_ref, s_ref, kaug_ref):
    seq = q_ref.shape[0]
    pair = pl.program_id(1)
    lane = lax.broadcasted_iota(jnp.int32, (TQ, LANES), 1)
    low = lane < FOX_HD
    kv_row = lax.broadcasted_iota(jnp.int32, (TK, TQ), 0)
    q_col = lax.broadcasted_iota(jnp.int32, (TK, TQ), 1)
    causal_t = jnp.concatenate([kv_row <= q_col] * 2, axis=1)

    src = lax.broadcasted_iota(jnp.int32, (LANES, LANES), 0)
    dst = lax.broadcasted_iota(jnp.int32, (LANES, LANES), 1)
    sel = jnp.zeros((LANES, LANES), F32)
    for hh in range(2):
        for term in range(3):
            hit = (dst == 3 * hh + term) & (src == FOX_HEADS * term + 2 * pair + hh)
            sel = jnp.where(hit, -1.0, sel)
    kaug_ref[:, 0:LANES] = k_ref[...]
    kaug_ref[:, LANES:] = _dot(cs_ref[...], sel.astype(BF16)).astype(BF16)
    q_ext = jnp.concatenate([jnp.where(lane < 3, 1.0, 0.0),
                             jnp.where((lane >= 3) & (lane < 6), 1.0, 0.0)], axis=0).astype(BF16)

    def pass1(qi):
        rows = slice(qi * TQ, (qi + 1) * TQ)
        qp = q_ref[rows, :]
        zero = jnp.zeros_like(qp)
        q_aug = jnp.concatenate(
            [jnp.concatenate([jnp.where(low, qp, zero), jnp.where(low, zero, qp)], axis=0), q_ext],
            axis=1)
        st_all = _dot_nt(kaug_ref[0:(qi + 1) * TK, :], q_aug)
        sbuf = s_ref.at[qi % 2]
        mrun = None
        for kb in range(qi + 1):
            krows = slice(kb * TK, (kb + 1) * TK)
            st = st_all[krows]
            if kb == qi:
                st = jnp.where(causal_t, st, -jnp.inf)
            sbuf[krows, :] = st
            bm = jnp.max(st.reshape(TK // SUBLANES, SUBLANES, 2 * TQ), axis=0)
            mrun = bm if mrun is None else jnp.maximum(mrun, bm)
            yield None
        cq = jnp.concatenate([ct_ref[pl.ds(2 * pair + hh, 1), rows] for hh in range(2)], axis=1)
        logit_max = jnp.max(mrun, axis=0, keepdims=True) + cq
        yield logit_max - cq

    def pass2(qi, shift):
        rows = slice(qi * TQ, (qi + 1) * TQ)
        sbuf = s_ref.at[qi % 2]
        acc = None
        lrun = None
        for kb in range(qi + 1):
            krows = slice(kb * TK, (kb + 1) * TK)
            pt = jnp.exp2(sbuf[krows, :] - shift)
            ps = jnp.sum(pt.reshape(TK // SUBLANES, SUBLANES, 2 * TQ), axis=0)
            lrun = ps if lrun is None else lrun + ps
            d = _dot(vt_ref[:, krows], pt.astype(BF16))
            acc = d if acc is None else acc + d
            yield None
        ot = acc / jnp.sum(lrun, axis=0, keepdims=True)
        o = jnp.concatenate([ot[:FOX_HD, :TQ], ot[FOX_HD:, TQ:]], axis=0).T
        out_ref[rows, :] = (o * sg_ref[rows, :].astype(F32)).astype(BF16)
        yield None

    def interleave(gen_a, gen_b):
        last_a = None
        while gen_a is not None or gen_b is not None:
            if gen_a is not None:
                try:
                    last_a = next(gen_a)
                except StopIteration:
                    gen_a = None
            if gen_b is not None:
                try:
                    next(gen_b)
                except StopIteration:
                    gen_b = None
        return last_a

    nq = seq // TQ
    shift = interleave(pass1(0), None)
    for qi in range(nq):
        shift = interleave(pass1(qi + 1) if qi + 1 < nq else None, pass2(qi, shift))


def _fox_attn(q, k, vt, cs, ct, sg, batch, seq):
    d = D_MODEL
    pairs = FOX_HEADS // 2
    blk = lambda b, j: (b, j)
    return pl.pallas_call(
        _fox_attn_kernel,
        grid=(batch, pairs),
        in_specs=[
            pl.BlockSpec((seq, LANES), blk),
            pl.BlockSpec((seq, LANES), blk),
            pl.BlockSpec((LANES, seq), lambda b, j: (b * pairs + j, 0)),
            pl.BlockSpec((seq, LANES), lambda b, j: (b, 0)),
            pl.BlockSpec((FOX_HEADS, seq), lambda b, j: (b, 0)),
            pl.BlockSpec((seq, LANES), blk),
        ],
        out_specs=pl.BlockSpec((seq, LANES), blk),
        out_shape=jax.ShapeDtypeStruct((batch * seq, d), BF16),
        scratch_shapes=[pltpu.VMEM((2, seq, 2 * TQ), F32), pltpu.VMEM((seq, 2 * LANES), BF16)],
        compiler_params=_params(("parallel", "parallel")),
        name="fox_attn",
    )(q, k, vt, cs, ct, sg)


def kernel(x, p, g_mix, g_ffn, g_ple, gla_w_in, gla_w_gk2, gla_b_gk2, gla_g_onorm, gla_w_out,
           kv_g_norm, kv_w_in, kv_g_knorm, kv_b_f, fox_w_in, fox_g_qnorm, fox_w_out, ffn_w_up,
           ffn_w_dw, ffn_b_dw, ffn_w_down, ple_w_gate, ple_b_gate, ple_w_proj, g_final):
    batch, seq, d = x.shape
    depth = p.shape[0]
    n_gla = gla_w_in.shape[0]
    x2d = x.reshape(batch * seq, d)
    p2d = p.reshape(depth, batch * seq, PLE_DIM)
    for i in range(depth):
        if i < n_gla:
            q, k, v, og, gk = _gla_proj(x2d, g_mix[i], gla_w_in[i], gla_w_gk2[i], gla_b_gk2[i])
            mix = _gla_scan(q, k, v, og, gk, gla_g_onorm[i], batch, seq)
            w_mix = gla_w_out[i]
        else:
            j = i - n_gla
            assert j == 0, "a single FoX layer is supported"
            q, sg, k_sh, vt_sh, cs_sh, ct_sh = _fox_proj(
                x2d, g_mix[i], kv_g_norm, fox_w_in[j], kv_w_in, kv_b_f, fox_g_qnorm[j],
                kv_g_knorm, batch, seq)
            mix = _fox_attn(q, k_sh, vt_sh, cs_sh, ct_sh, sg, batch, seq)
            w_mix = fox_w_out[j]
        x2d = _ffn_ple(x2d, mix, w_mix, p2d[i], g_ffn[i], ffn_w_up[i], ffn_w_dw[i], ffn_b_dw[i],
                       ffn_w_down[i], g_ple[i], ple_w_gate[i], ple_b_gate[i], ple_w_proj[i],
                       g_final, batch, seq, final_norm=(i == depth - 1))
    return x2d.reshape(batch, seq, d)
```

```python
import functools
import math

import jax
import jax.numpy as jnp
from jax import lax
from jax.experimental import pallas as pl
from jax.experimental.pallas import tpu as pltpu

F32 = jnp.float32
BF16 = jnp.bfloat16

EPS = 1e-6
D_MODEL = 1024
GLA_HEADS = 4
GLA_DK = 128
GLA_DV = 256
GLA_QK = GLA_HEADS * GLA_DK
GLA_V = GLA_HEADS * GLA_DV
GLA_RANK = 16
GLA_TAU = 16.0
GLA_CHUNK = 64
FOX_HEADS = 16
FOX_HD = 64
D_FF = 2816
PLE_DIM = 256
LOG2E = math.log2(math.e)

LANES = 128
SUBLANES = 8
VMEM_LIMIT = 56 * 1024 * 1024

TM_PROJ = 512
TM_GLA = 512
TM_FFN = 512
FFN_CHUNK = 256
TQ = 256
TK = 256


def _dot(a, b):
    return jnp.dot(a, b, preferred_element_type=F32)


def _dot_nt(a, b):
    return lax.dot_general(a, b, (((1,), (1,)), ((), ())), preferred_element_type=F32)


def _dot_tn(a, b):
    return lax.dot_general(a, b, (((0,), (0,)), ((), ())), preferred_element_type=F32)


def _split3(x):
    hi = x.astype(BF16)
    r = x - hi.astype(F32)
    mid = r.astype(BF16)
    lo = (r - mid.astype(F32)).astype(BF16)
    return hi, mid, lo


def _rms(x, g):
    return x * lax.rsqrt(jnp.mean(x * x, axis=-1, keepdims=True) + EPS) * g


def _sigmoid(z):
    return 1.0 / (1.0 + jnp.exp(-z))


def _log_sigmoid(z):
    return jnp.minimum(z, 0.0) - jnp.log1p(jnp.exp(-jnp.abs(z)))


def _const_spec(shape):
    nd = len(shape)
    return pl.BlockSpec(shape, lambda *_: (0,) * nd, pipeline_mode=pl.Buffered(1))


def _params(sem):
    return pltpu.CompilerParams(dimension_semantics=sem, vmem_limit_bytes=VMEM_LIMIT)


def _gla_proj_kernel(x_ref, g_ref, w_ref, wg2_ref, bg2_ref, q_ref, k_ref, v_ref, og_ref, gk_ref):
    hn = _rms(x_ref[...], g_ref[...]).astype(BF16)
    q_ref[...] = _dot(hn, w_ref[:, 0:GLA_QK]).astype(BF16)
    k_ref[...] = _dot(hn, w_ref[:, GLA_QK:2 * GLA_QK]).astype(BF16)
    v_ref[...] = _dot(hn, w_ref[:, 2 * GLA_QK:2 * GLA_QK + GLA_V]).astype(BF16)
    og_ref[...] = _dot(hn, w_ref[:, 2 * GLA_QK + GLA_V:2 * GLA_QK + 2 * GLA_V]).astype(BF16)
    gr = _dot(hn, w_ref[:, 2 * GLA_QK + 2 * GLA_V:]).astype(BF16)
    z = _dot(gr, wg2_ref[...]) + bg2_ref[...]
    gk_ref[...] = _log_sigmoid(z) * (1.0 / GLA_TAU)


def _gla_proj(x2d, g, w_in, w_gk2, b_gk2):
    t = x2d.shape[0]
    tm = TM_PROJ
    n_main = 2 * GLA_QK + 2 * GLA_V
    w_pad = jnp.zeros((D_MODEL, n_main + LANES), BF16).at[:, :n_main + GLA_RANK].set(w_in.astype(BF16))
    wg2_pad = jnp.zeros((LANES, GLA_QK), BF16).at[:GLA_RANK].set(w_gk2.astype(BF16))
    row = lambda i: (i, 0)
    return pl.pallas_call(
        _gla_proj_kernel,
        grid=(t // tm,),
        in_specs=[
            pl.BlockSpec((tm, D_MODEL), row),
            _const_spec((1, D_MODEL)),
            _const_spec((D_MODEL, n_main + LANES)),
            _const_spec((LANES, GLA_QK)),
            _const_spec((1, GLA_QK)),
        ],
        out_specs=[
            pl.BlockSpec((tm, GLA_QK), row),
            pl.BlockSpec((tm, GLA_QK), row),
            pl.BlockSpec((tm, GLA_V), row),
            pl.BlockSpec((tm, GLA_V), row),
            pl.BlockSpec((tm, GLA_QK), row),
        ],
        out_shape=[
            jax.ShapeDtypeStruct((t, GLA_QK), BF16),
            jax.ShapeDtypeStruct((t, GLA_QK), BF16),
            jax.ShapeDtypeStruct((t, GLA_V), BF16),
            jax.ShapeDtypeStruct((t, GLA_V), BF16),
            jax.ShapeDtypeStruct((t, GLA_QK), F32),
        ],
        compiler_params=_params(("parallel",)),
        name="gla_proj",
    )(x2d, g.reshape(1, D_MODEL), w_pad, wg2_pad, b_gk2.reshape(1, GLA_QK))


def _gla_scan_kernel(q_ref, k_ref, v_ref, og_ref, gk_ref, gon_ref, out_ref, state_ref, o_ref):
    c = GLA_CHUNK
    tm = q_ref.shape[0]

    @pl.when(pl.program_id(1) == 0)
    def _():
        state_ref[...] = jnp.zeros_like(state_ref)

    rowi = lax.broadcasted_iota(jnp.int32, (c, c), 0)
    coli = lax.broadcasted_iota(jnp.int32, (c, c), 1)
    tril = rowi >= coli
    tril_bf = jnp.where(tril, 1.0, 0.0).astype(BF16)
    scale = GLA_DK ** -0.5

    nchunk = tm // c
    chunk_rows = [slice(ci * c, (ci + 1) * c) for ci in range(nchunk)]
    head_k = [slice(h * GLA_DK, (h + 1) * GLA_DK) for h in range(GLA_HEADS)]
    head_v = [slice(h * GLA_DV, (h + 1) * GLA_DV) for h in range(GLA_HEADS)]

    hi, mid, lo = _split3(gk_ref[...])
    bcum = jnp.concatenate(
        [_dot(tril_bf, hi[r]) + _dot(tril_bf, mid[r]) + _dot(tril_bf, lo[r]) for r in chunk_rows],
        axis=0)
    b_last = [bcum[r.stop - 1:r.stop, :] for r in chunk_rows]
    b_last_rows = jnp.concatenate([jnp.broadcast_to(b, (c, GLA_QK)) for b in b_last], axis=0)
    q = q_ref[...].astype(F32) * scale
    k = k_ref[...].astype(F32)
    q_in = (q * jnp.exp(bcum)).astype(BF16)
    k_in = (k * jnp.exp(-bcum)).astype(BF16)
    k_end = (k * jnp.exp(b_last_rows - bcum)).astype(BF16)
    decay = jnp.exp(jnp.concatenate(b_last, axis=0))
    pad = jnp.zeros((LANES - nchunk, GLA_DK), F32)
    decay_t = [jnp.concatenate([decay[:, ks], pad], axis=0).T for ks in head_k]

    att = [[jnp.where(tril, _dot_nt(q_in[r, ks], k_in[r, ks]), 0.0).astype(BF16) for ks in head_k]
           for r in chunk_rows]
    upd = [[_dot_tn(k_end[r, head_k[h]], v_ref[r, head_v[h]]) for h in range(GLA_HEADS)]
           for r in chunk_rows]

    state = [state_ref[h] for h in range(GLA_HEADS)]
    for ci, r in enumerate(chunk_rows):
        for h in range(GLA_HEADS):
            o_ref[r, head_v[h]] = (_dot(att[ci][h], v_ref[r, head_v[h]])
                                   + _dot(q_in[r, head_k[h]], state[h].astype(BF16)))
            state[h] = decay_t[h][:, ci:ci + 1] * state[h] + upd[ci][h]
    for h in range(GLA_HEADS):
        state_ref[h] = state[h]

    for h in range(GLA_HEADS):
        vs = slice(h * GLA_DV, (h + 1) * GLA_DV)
        on = _rms(o_ref[:, vs], gon_ref[...])
        og = og_ref[:, vs].astype(F32)
        out_ref[:, vs] = (on * (og * _sigmoid(og))).astype(BF16)


def _gla_scan(q, k, v, og, gk, g_onorm, batch, seq):
    tm = TM_GLA
    nt = seq // tm
    row = lambda b, s: (b * nt + s, 0)
    return pl.pallas_call(
        _gla_scan_kernel,
        grid=(batch, nt),
        in_specs=[
            pl.BlockSpec((tm, GLA_QK), row),
            pl.BlockSpec((tm, GLA_QK), row),
            pl.BlockSpec((tm, GLA_V), row),
            pl.BlockSpec((tm, GLA_V), row),
            pl.BlockSpec((tm, GLA_QK), row),
            _const_spec((1, GLA_DV)),
        ],
        out_specs=pl.BlockSpec((tm, GLA_V), row),
        out_shape=jax.ShapeDtypeStruct((batch * seq, GLA_V), BF16),
        scratch_shapes=[
            pltpu.VMEM((GLA_HEADS, GLA_DK, GLA_DV), F32),
            pltpu.VMEM((tm, GLA_V), F32),
        ],
        compiler_params=_params(("parallel", "arbitrary")),
        name="gla_scan",
    )(q, k, v, og, gk, g_onorm.reshape(1, GLA_DV))


def _ffn_kernel(x_ref, mix_ref, wmix_ref, p_ref, gf_ref, wup_ref, wdw_ref, bdw_ref, wdn_ref, gp_ref,
                wg_ref, bg_ref, wp_ref, gfin_ref, out_ref, carry_ref, act_ref, *, final_norm):
    tm = x_ref.shape[0]
    fc = FFN_CHUNK
    nch = D_FF // fc
    hdr = SUBLANES

    @pl.when(pl.program_id(1) == 0)
    def _():
        carry_ref[...] = jnp.zeros_like(carry_ref)

    x = x_ref[...] + _dot(mix_ref[...], wmix_ref[...])
    hn = _rms(x, gf_ref[...]).astype(BF16)

    def conv(col0, slot):
        u = _dot(hn, wup_ref[:, col0:col0 + fc])
        ext = jnp.concatenate([carry_ref[slot], u], axis=0)
        carry_ref[slot] = u[tm - hdr:tm, :]
        u1 = ext[hdr - 1:hdr - 1 + tm, :]
        u2 = ext[hdr - 2:hdr - 2 + tm, :]
        w = wdw_ref[:, col0:col0 + fc]
        return w[0:1] * u2 + w[1:2] * u1 + w[2:3] * u + bdw_ref[:, col0:col0 + fc]

    for ci in range(nch):
        a = conv(ci * fc, 2 * ci)
        b = conv(D_FF + ci * fc, 2 * ci + 1)
        act_ref[:, ci * fc:(ci + 1) * fc] = (a * _sigmoid(a) * b).astype(BF16)
    y = x + _dot(act_ref[...], wdn_ref[...])
    hp = _rms(y, gp_ref[...]).astype(BF16)
    gate = _sigmoid(_dot(hp, wg_ref[...]) + bg_ref[...])
    pe = _dot(p_ref[...].astype(BF16), wp_ref[...])
    y = y + pe * gate
    if final_norm:
        y = _rms(y, gfin_ref[...])
    out_ref[...] = y


def _ffn_ple(x2d, mix, w_mix, p3d, layer, g_ffn, w_up, w_dw, b_dw, w_down, g_ple, w_gate, b_gate,
             w_proj, g_final, batch, seq, final_norm):
    tm = TM_FFN
    nt = seq // tm
    nslots = 2 * (D_FF // FFN_CHUNK)
    row = lambda b, s: (b * nt + s, 0)
    return pl.pallas_call(
        functools.partial(_ffn_kernel, final_norm=final_norm),
        grid=(batch, nt),
        in_specs=[
            pl.BlockSpec((tm, D_MODEL), row),
            pl.BlockSpec((tm, D_MODEL), row),
            _const_spec((D_MODEL, D_MODEL)),
            pl.BlockSpec((None, tm, PLE_DIM), lambda b, s: (layer, b * nt + s, 0)),
            _const_spec((1, D_MODEL)),
            _const_spec((D_MODEL, 2 * D_FF)),
            _const_spec((3, 2 * D_FF)),
            _const_spec((1, 2 * D_FF)),
            _const_spec((D_FF, D_MODEL)),
            _const_spec((1, D_MODEL)),
            _const_spec((D_MODEL, D_MODEL)),
            _const_spec((1, D_MODEL)),
            _const_spec((PLE_DIM, D_MODEL)),
            _const_spec((1, D_MODEL)),
        ],
        out_specs=pl.BlockSpec((tm, D_MODEL), row),
        out_shape=jax.ShapeDtypeStruct(x2d.shape, F32),
        scratch_shapes=[
            pltpu.VMEM((nslots, SUBLANES, FFN_CHUNK), F32),
            pltpu.VMEM((tm, D_FF), BF16),
        ],
        compiler_params=_params(("parallel", "arbitrary")),
        name="ffn_ple",
    )(x2d, mix, w_mix.astype(BF16), p3d, g_ffn.reshape(1, -1), w_up.astype(BF16), w_dw,
      b_dw.reshape(1, -1), w_down.astype(BF16), g_ple.reshape(1, -1), w_gate.astype(BF16),
      b_gate.reshape(1, -1), w_proj.astype(BF16), g_final.reshape(1, -1))


def _fox_proj_kernel(x_ref, gm_ref, gkv_ref, wq_ref, wog_ref, wk_ref, wvft_ref, bf_ref,
                     gq_ref, gkn_ref, ones_ref, triu_ref,
                     q_ref, sg_ref, k_ref, vt_ref, cs_ref, ct_ref, ccol_ref):
    tm = x_ref.shape[0]

    @pl.when(pl.program_id(1) == 0)
    def _():
        ccol_ref[...] = jnp.zeros_like(ccol_ref)

    x = x_ref[...]
    xn = x * lax.rsqrt(jnp.mean(x * x, axis=-1, keepdims=True) + EPS)
    h1 = (xn * gm_ref[...]).astype(BF16)
    h2 = (xn * gkv_ref[...]).astype(BF16)

    def head_norm(t, g):
        t2 = (t * t).astype(BF16)
        blk = 2 * LANES
        ss = jnp.concatenate(
            [_dot(t2[:, i * blk:(i + 1) * blk], ones_ref[...]) for i in range(D_MODEL // blk)], axis=1)
        return t * lax.rsqrt(ss * (1.0 / FOX_HD) + EPS) * g

    q = _dot(h1, wq_ref[...])
    q_ref[...] = (head_norm(q, gq_ref[...]) * (FOX_HD ** -0.5 * LOG2E)).astype(BF16)
    sg_ref[...] = _sigmoid(_dot(h1, wog_ref[...])).astype(BF16)
    k = _dot(h2, wk_ref[...])
    k_ref[...] = head_norm(k, gkn_ref[...]).astype(BF16)
    vft = _dot_nt(wvft_ref[...], h2)
    vt_ref[...] = vft[0:D_MODEL].astype(BF16)

    lf = _log_sigmoid(vft[D_MODEL:] + bf_ref[:, 0:1])
    hi, mid, lo = _split3(lf)
    triu = triu_ref[...]
    ct = _dot(hi, triu) + _dot(mid, triu) + _dot(lo, triu) + ccol_ref[:, 0:1]
    ccol_ref[...] = jnp.broadcast_to(ct[:, tm - 1:tm], ccol_ref.shape)
    ct2 = ct * LOG2E
    ct_ref[...] = ct2
    c2 = jnp.concatenate([ct2, jnp.zeros((LANES - FOX_HEADS, tm), F32)], axis=0).T
    hi = c2.astype(BF16).astype(F32)
    r = c2 - hi
    mid = r.astype(BF16).astype(F32)
    lo = r - mid
    lane = lax.broadcasted_iota(jnp.int32, c2.shape, 1)
    cs_ref[...] = jnp.where(
        lane < FOX_HEADS, hi,
        jnp.where(lane < 2 * FOX_HEADS, pltpu.roll(mid, FOX_HEADS, 1),
                  pltpu.roll(lo, 2 * FOX_HEADS, 1))).astype(BF16)


def _fox_proj(x2d, g_mix, g_kv, w_fox_in, w_kv_in, b_f, g_qnorm, g_knorm, batch, seq):
    tm = TM_PROJ
    nt = seq // tm
    d = D_MODEL
    wq = w_fox_in[:, :d].astype(BF16)
    wog = w_fox_in[:, d:].astype(BF16)
    wk = w_kv_in[:, :d].astype(BF16)
    wvft = w_kv_in[:, d:].T.astype(BF16)
    bf_col = jnp.broadcast_to(b_f[:, None], (FOX_HEADS, LANES))
    gq = jnp.tile(g_qnorm, FOX_HEADS).reshape(1, d)
    gkn = jnp.tile(g_knorm, FOX_HEADS).reshape(1, d)
    blk = 2 * LANES
    hid = jnp.arange(blk) // FOX_HD
    ones_bd = (hid[:, None] == hid[None, :]).astype(BF16)
    tid = jnp.arange(tm)
    triu = (tid[:, None] <= tid[None, :]).astype(BF16)
    row = lambda b, s: (b * nt + s, 0)
    return pl.pallas_call(
        _fox_proj_kernel,
        grid=(batch, nt),
        in_specs=[
            pl.BlockSpec((tm, d), row),
            _const_spec((1, d)), _const_spec((1, d)),
            _const_spec((d, d)), _const_spec((d, d)), _const_spec((d, d)),
            _const_spec((d + FOX_HEADS, d)), _const_spec((FOX_HEADS, LANES)),
            _const_spec((1, d)), _const_spec((1, d)),
            _const_spec((blk, blk)), _const_spec((tm, tm)),
        ],
        out_specs=[
            pl.BlockSpec((tm, d), row),
            pl.BlockSpec((tm, d), row),
            pl.BlockSpec((tm, d), row),
            pl.BlockSpec((d, tm), lambda b, s: (b, s)),
            pl.BlockSpec((tm, LANES), row),
            pl.BlockSpec((FOX_HEADS, tm), lambda b, s: (b, s)),
        ],
        out_shape=[
            jax.ShapeDtypeStruct((batch * seq, d), BF16),
            jax.ShapeDtypeStruct((batch * seq, d), BF16),
            jax.ShapeDtypeStruct((batch * seq, d), BF16),
            jax.ShapeDtypeStruct((batch * d, seq), BF16),
            jax.ShapeDtypeStruct((batch * seq, LANES), BF16),
            jax.ShapeDtypeStruct((batch * FOX_HEADS, seq), F32),
        ],
        scratch_shapes=[pltpu.VMEM((FOX_HEADS, LANES), F32)],
        compiler_params=_params(("parallel", "arbitrary")),
        name="fox_proj",
    )(x2d, g_mix.reshape(1, d), g_kv.reshape(1, d), wq, wog, wk, wvft, bf_col, gq, gkn,
      ones_bd, triu)


def _fox_attn_kernel(q_ref, k_ref, vt_ref, cs_ref, ct_ref, sg_ref, out_ref, s_ref, kaug_ref):
    seq = q_ref.shape[0]
    pair = pl.program_id(1)
    lane = lax.broadcasted_iota(jnp.int32, (TQ, LANES), 1)
    low = lane < FOX_HD
    kv_row = lax.broadcasted_iota(jnp.int32, (TK, TQ), 0)
    q_col = lax.broadcasted_iota(jnp.int32, (TK, TQ), 1)

    c_heads = [ct_ref[pl.ds(2 * pair + hh, 1), :] for hh in range(2)]

    def key_blocks(qi):
        return ((qi + 1) * TQ + TK - 1) // TK

    kaug_ref[:, 0:LANES] = k_ref[...]
    kaug_ref[:, LANES:] = cs_ref[...]
    q_ext = jnp.concatenate(
        [sum(jnp.where(lane == FOX_HEADS * term + 2 * pair + hh, -1.0, 0.0) for term in range(3))
         for hh in range(2)], axis=0).astype(BF16)

    def pass1(qi):
        rows = slice(qi * TQ, (qi + 1) * TQ)
        qp = q_ref[rows, :]
        zero = jnp.zeros_like(qp)
        q_aug = jnp.concatenate(
            [jnp.concatenate([jnp.where(low, qp, zero), jnp.where(low, zero, qp)], axis=0), q_ext],
            axis=1)
        nkb = key_blocks(qi)
        st_all = _dot_nt(kaug_ref[0:nkb * TK, :], q_aug)
        sbuf = s_ref.at[qi % 2]
        mrun = None
        for kb in range(nkb):
            krows = slice(kb * TK, (kb + 1) * TK)
            st = st_all[krows]
            if (kb + 1) * TK > qi * TQ + 1:
                causal_t = kv_row + (kb * TK - qi * TQ) <= q_col
                st = jnp.where(jnp.concatenate([causal_t] * 2, axis=1), st, -jnp.inf)
            sbuf[krows, :] = st
            bm = jnp.max(st.reshape(TK // SUBLANES, SUBLANES, 2 * TQ), axis=0)
            mrun = bm if mrun is None else jnp.maximum(mrun, bm)
        cq = jnp.concatenate([c_heads[hh][:, rows] for hh in range(2)], axis=1)
        logit_max = jnp.max(mrun, axis=0, keepdims=True) + cq
        return logit_max - cq

    def pass2(qi, shift):
        rows = slice(qi * TQ, (qi + 1) * TQ)
        sbuf = s_ref.at[qi % 2]
        acc = None
        lrun = None
        for kb in range(key_blocks(qi)):
            krows = slice(kb * TK, (kb + 1) * TK)
            pt = jnp.exp2(sbuf[krows, :] - shift)
            ps = jnp.sum(pt.reshape(TK // SUBLANES, SUBLANES, 2 * TQ), axis=0)
            lrun = ps if lrun is None else lrun + ps
            d = _dot(vt_ref[:, krows], pt.astype(BF16))
            acc = d if acc is None else acc + d
        ot = acc / jnp.sum(lrun, axis=0, keepdims=True)
        o = jnp.concatenate([ot[:FOX_HD, :TQ], ot[FOX_HD:, TQ:]], axis=0).T
        out_ref[rows, :] = (o * sg_ref[rows, :].astype(F32)).astype(BF16)

    nq = seq // TQ
    shift = pass1(0)
    for qi in range(nq):
        nxt = pass1(qi + 1) if qi + 1 < nq else None
        pass2(qi, shift)
        shift = nxt


def _fox_attn(q, k, vt, cs, ct, sg, batch, seq):
    d = D_MODEL
    pairs = FOX_HEADS // 2
    blk = lambda b, j: (b, j)
    return pl.pallas_call(
        _fox_attn_kernel,
        grid=(batch, pairs),
        in_specs=[
            pl.BlockSpec((seq, LANES), blk),
            pl.BlockSpec((seq, LANES), blk),
            pl.BlockSpec((LANES, seq), lambda b, j: (b * pairs + j, 0)),
            pl.BlockSpec((seq, LANES), lambda b, j: (b, 0)),
            pl.BlockSpec((FOX_HEADS, seq), lambda b, j: (b, 0)),
            pl.BlockSpec((seq, LANES), blk),
        ],
        out_specs=pl.BlockSpec((seq, LANES), blk),
        out_shape=jax.ShapeDtypeStruct((batch * seq, d), BF16),
        scratch_shapes=[pltpu.VMEM((2, seq, 2 * TQ), F32), pltpu.VMEM((seq, 2 * LANES), BF16)],
        compiler_params=_params(("parallel", "parallel")),
        name="fox_attn",
    )(q, k, vt, cs, ct, sg)


def kernel(x, p, g_mix, g_ffn, g_ple, gla_w_in, gla_w_gk2, gla_b_gk2, gla_g_onorm, gla_w_out,
           kv_g_norm, kv_w_in, kv_g_knorm, kv_b_f, fox_w_in, fox_g_qnorm, fox_w_out, ffn_w_up,
           ffn_w_dw, ffn_b_dw, ffn_w_down, ple_w_gate, ple_b_gate, ple_w_proj, g_final):
    batch, seq, d = x.shape
    depth = p.shape[0]
    n_gla = gla_w_in.shape[0]
    x2d = x.reshape(batch * seq, d)
    p3d = p.reshape(depth, batch * seq, PLE_DIM)
    for i in range(depth):
        if i < n_gla:
            q, k, v, og, gk = _gla_proj(x2d, g_mix[i], gla_w_in[i], gla_w_gk2[i], gla_b_gk2[i])
            mix = _gla_scan(q, k, v, og, gk, gla_g_onorm[i], batch, seq)
            w_mix = gla_w_out[i]
        else:
            j = i - n_gla
            assert j == 0, "a single FoX layer is supported"
            q, sg, k_sh, vt_sh, cs_sh, ct_sh = _fox_proj(
                x2d, g_mix[i], kv_g_norm, fox_w_in[j], kv_w_in, kv_b_f, fox_g_qnorm[j],
                kv_g_knorm, batch, seq)
            mix = _fox_attn(q, k_sh, vt_sh, cs_sh, ct_sh, sg, batch, seq)
            w_mix = fox_w_out[j]
        x2d = _ffn_ple(x2d, mix, w_mix, p3d, i, g_ffn[i], ffn_w_up[i], ffn_w_dw[i], ffn_b_dw[i],
                       ffn_w_down[i], g_ple[i], ple_w_gate[i], ple_b_gate[i], ple_w_proj[i],
                       g_final, batch, seq, final_norm=(i == depth - 1))
    return x2d.reshape(batch, seq, d)
```

```python
import functools
import math

import jax
import jax.numpy as jnp
from jax import lax
from jax.experimental import pallas as pl
from jax.experimental.pallas import tpu as pltpu

F32 = jnp.float32
BF16 = jnp.bfloat16

EPS = 1e-6
D_MODEL = 1024
GLA_HEADS = 4
GLA_DK = 128
GLA_DV = 256
GLA_QK = GLA_HEADS * GLA_DK
GLA_V = GLA_HEADS * GLA_DV
GLA_RANK = 16
GLA_TAU = 16.0
GLA_CHUNK = 64
FOX_HEADS = 16
FOX_HD = 64
D_FF = 2816
PLE_DIM = 256
LOG2E = math.log2(math.e)

LANES = 128
SUBLANES = 8
VMEM_LIMIT = 56 * 1024 * 1024

TM_PROJ = 1024
TM_GLA = 1024
TM_FFN = 512
FFN_CHUNK = 256
TQ = 256
TK = 256


def _dot(a, b):
    return jnp.dot(a, b, preferred_element_type=F32)


def _dot_nt(a, b):
    return lax.dot_general(a, b, (((1,), (1,)), ((), ())), preferred_element_type=F32)


def _dot_tn(a, b):
    return lax.dot_general(a, b, (((0,), (0,)), ((), ())), preferred_element_type=F32)


def _split3(x):
    hi = x.astype(BF16)
    r = x - hi.astype(F32)
    mid = r.astype(BF16)
    lo = (r - mid.astype(F32)).astype(BF16)
    return hi, mid, lo


def _rms(x, g):
    return x * lax.rsqrt(jnp.mean(x * x, axis=-1, keepdims=True) + EPS) * g


def _sigmoid(z):
    return 1.0 / (1.0 + jnp.exp(-z))


def _log_sigmoid(z):
    return jnp.minimum(z, 0.0) - jnp.log1p(jnp.exp(-jnp.abs(z)))


def _const_spec(shape):
    nd = len(shape)
    return pl.BlockSpec(shape, lambda *_: (0,) * nd, pipeline_mode=pl.Buffered(1))


def _params(sem):
    return pltpu.CompilerParams(dimension_semantics=sem, vmem_limit_bytes=VMEM_LIMIT)


def _gla_kernel(x_ref, g_ref, w_ref, wg2_ref, bg2_ref, gon_ref, out_ref, state_ref, o_ref):
    c = GLA_CHUNK
    tm = x_ref.shape[0]

    @pl.when(pl.program_id(1) == 0)
    def _():
        state_ref[...] = jnp.zeros_like(state_ref)

    hn = _rms(x_ref[...], g_ref[...]).astype(BF16)
    q = _dot(hn, w_ref[:, 0:GLA_QK]).astype(BF16)
    k = _dot(hn, w_ref[:, GLA_QK:2 * GLA_QK]).astype(BF16)
    v = _dot(hn, w_ref[:, 2 * GLA_QK:2 * GLA_QK + GLA_V]).astype(BF16)
    og = _dot(hn, w_ref[:, 2 * GLA_QK + GLA_V:2 * GLA_QK + 2 * GLA_V]).astype(BF16)
    gr = _dot(hn, w_ref[:, 2 * GLA_QK + 2 * GLA_V:]).astype(BF16)
    gk = _log_sigmoid(_dot(gr, wg2_ref[...]) + bg2_ref[...]) * (1.0 / GLA_TAU)

    rowi = lax.broadcasted_iota(jnp.int32, (c, c), 0)
    coli = lax.broadcasted_iota(jnp.int32, (c, c), 1)
    tril = rowi >= coli
    tril_bf = jnp.where(tril, 1.0, 0.0).astype(BF16)
    scale = GLA_DK ** -0.5

    nchunk = tm // c
    chunk_rows = [slice(ci * c, (ci + 1) * c) for ci in range(nchunk)]
    head_k = [slice(h * GLA_DK, (h + 1) * GLA_DK) for h in range(GLA_HEADS)]
    head_v = [slice(h * GLA_DV, (h + 1) * GLA_DV) for h in range(GLA_HEADS)]

    hi, mid, lo = _split3(gk)
    bcum = jnp.concatenate(
        [_dot(tril_bf, hi[r]) + _dot(tril_bf, mid[r]) + _dot(tril_bf, lo[r]) for r in chunk_rows],
        axis=0)
    b_last = [bcum[r.stop - 1:r.stop, :] for r in chunk_rows]
    b_last_rows = jnp.concatenate([jnp.broadcast_to(b, (c, GLA_QK)) for b in b_last], axis=0)
    qf = q.astype(F32) * scale
    kf = k.astype(F32)
    q_in = (qf * jnp.exp(bcum)).astype(BF16)
    k_in = (kf * jnp.exp(-bcum)).astype(BF16)
    k_end = (kf * jnp.exp(b_last_rows - bcum)).astype(BF16)
    decay = jnp.exp(jnp.concatenate(b_last, axis=0))
    pad = jnp.zeros((LANES - nchunk, GLA_DK), F32)
    decay_t = [jnp.concatenate([decay[:, ks], pad], axis=0).T for ks in head_k]

    att = [[jnp.where(tril, _dot_nt(q_in[r, ks], k_in[r, ks]), 0.0).astype(BF16) for ks in head_k]
           for r in chunk_rows]
    upd = [[_dot_tn(k_end[r, head_k[h]], v[r, head_v[h]]) for h in range(GLA_HEADS)]
           for r in chunk_rows]

    state = [state_ref[h] for h in range(GLA_HEADS)]
    for ci, r in enumerate(chunk_rows):
        for h in range(GLA_HEADS):
            o_ref[r, head_v[h]] = (_dot(att[ci][h], v[r, head_v[h]])
                                   + _dot(q_in[r, head_k[h]], state[h].astype(BF16)))
            state[h] = decay_t[h][:, ci:ci + 1] * state[h] + upd[ci][h]
    for h in range(GLA_HEADS):
        state_ref[h] = state[h]

    for h in range(GLA_HEADS):
        vs = head_v[h]
        on = _rms(o_ref[:, vs], gon_ref[...])
        og_h = og[:, vs].astype(F32)
        out_ref[:, vs] = (on * (og_h * _sigmoid(og_h))).astype(BF16)


def _gla_mixer(x2d, g, w_in, w_gk2, b_gk2, g_onorm, batch, seq):
    tm = TM_GLA
    nt = seq // tm
    n_main = 2 * GLA_QK + 2 * GLA_V
    w_pad = jnp.zeros((D_MODEL, n_main + LANES), BF16).at[:, :n_main + GLA_RANK].set(w_in.astype(BF16))
    wg2_pad = jnp.zeros((LANES, GLA_QK), BF16).at[:GLA_RANK].set(w_gk2.astype(BF16))
    row = lambda b, s: (b * nt + s, 0)
    return pl.pallas_call(
        _gla_kernel,
        grid=(batch, nt),
        in_specs=[
            pl.BlockSpec((tm, D_MODEL), row),
            _const_spec((1, D_MODEL)),
            _const_spec((D_MODEL, n_main + LANES)),
            _const_spec((LANES, GLA_QK)),
            _const_spec((1, GLA_QK)),
            _const_spec((1, GLA_DV)),
        ],
        out_specs=pl.BlockSpec((tm, GLA_V), row),
        out_shape=jax.ShapeDtypeStruct((batch * seq, GLA_V), BF16),
        scratch_shapes=[
            pltpu.VMEM((GLA_HEADS, GLA_DK, GLA_DV), F32),
            pltpu.VMEM((tm, GLA_V), F32),
        ],
        compiler_params=_params(("parallel", "arbitrary")),
        name="gla_mixer",
    )(x2d, g.reshape(1, D_MODEL), w_pad, wg2_pad, b_gk2.reshape(1, GLA_QK),
      g_onorm.reshape(1, GLA_DV))


def _ffn_kernel(x_ref, mix_ref, wmix_ref, p_ref, gf_ref, wup_ref, wdw_ref, bdw_ref, wdn_ref, gp_ref,
                wg_ref, bg_ref, wp_ref, gfin_ref, out_ref, carry_ref, act_ref, *, final_norm):
    tm = x_ref.shape[0]
    fc = FFN_CHUNK
    nch = D_FF // fc
    hdr = SUBLANES

    @pl.when(pl.program_id(1) == 0)
    def _():
        carry_ref[...] = jnp.zeros_like(carry_ref)

    x = x_ref[...] + _dot(mix_ref[...], wmix_ref[...])
    hn = _rms(x, gf_ref[...]).astype(BF16)

    def conv(col0, slot):
        u = _dot(hn, wup_ref[:, col0:col0 + fc])
        ext = jnp.concatenate([carry_ref[slot], u], axis=0)
        carry_ref[slot] = u[tm - hdr:tm, :]
        u1 = ext[hdr - 1:hdr - 1 + tm, :]
        u2 = ext[hdr - 2:hdr - 2 + tm, :]
        w = wdw_ref[:, col0:col0 + fc]
        return w[0:1] * u2 + w[1:2] * u1 + w[2:3] * u + bdw_ref[:, col0:col0 + fc]

    for ci in range(nch):
        a = conv(ci * fc, 2 * ci)
        b = conv(D_FF + ci * fc, 2 * ci + 1)
        act_ref[:, ci * fc:(ci + 1) * fc] = (a * _sigmoid(a) * b).astype(BF16)
    y = x + _dot(act_ref[...], wdn_ref[...])
    hp = _rms(y, gp_ref[...]).astype(BF16)
    gate = _sigmoid(_dot(hp, wg_ref[...]) + bg_ref[...])
    pe = _dot(p_ref[...].astype(BF16), wp_ref[...])
    y = y + pe * gate
    if final_norm:
        y = _rms(y, gfin_ref[...])
    out_ref[...] = y


def _ffn_ple(x2d, mix, w_mix, p3d, layer, g_ffn, w_up, w_dw, b_dw, w_down, g_ple, w_gate, b_gate,
             w_proj, g_final, batch, seq, final_norm):
    tm = TM_FFN
    nt = seq // tm
    nslots = 2 * (D_FF // FFN_CHUNK)
    row = lambda b, s: (b * nt + s, 0)
    return pl.pallas_call(
        functools.partial(_ffn_kernel, final_norm=final_norm),
        grid=(batch, nt),
        in_specs=[
            pl.BlockSpec((tm, D_MODEL), row),
            pl.BlockSpec((tm, D_MODEL), row),
            _const_spec((D_MODEL, D_MODEL)),
            pl.BlockSpec((None, tm, PLE_DIM), lambda b, s: (layer, b * nt + s, 0)),
            _const_spec((1, D_MODEL)),
            _const_spec((D_MODEL, 2 * D_FF)),
            _const_spec((3, 2 * D_FF)),
            _const_spec((1, 2 * D_FF)),
            _const_spec((D_FF, D_MODEL)),
            _const_spec((1, D_MODEL)),
            _const_spec((D_MODEL, D_MODEL)),
            _const_spec((1, D_MODEL)),
            _const_spec((PLE_DIM, D_MODEL)),
            _const_spec((1, D_MODEL)),
        ],
        out_specs=pl.BlockSpec((tm, D_MODEL), row),
        out_shape=jax.ShapeDtypeStruct(x2d.shape, F32),
        scratch_shapes=[
            pltpu.VMEM((nslots, SUBLANES, FFN_CHUNK), F32),
            pltpu.VMEM((tm, D_FF), BF16),
        ],
        compiler_params=_params(("parallel", "arbitrary")),
        name="ffn_ple",
    )(x2d, mix, w_mix.astype(BF16), p3d, g_ffn.reshape(1, -1), w_up.astype(BF16), w_dw,
      b_dw.reshape(1, -1), w_down.astype(BF16), g_ple.reshape(1, -1), w_gate.astype(BF16),
      b_gate.reshape(1, -1), w_proj.astype(BF16), g_final.reshape(1, -1))


def _fox_proj_kernel(x_ref, gm_ref, gkv_ref, wq_ref, wog_ref, wk_ref, wvft_ref, bf_ref,
                     gq_ref, gkn_ref, ones_ref, triu_ref,
                     q_ref, sg_ref, k_ref, vt_ref, cs_ref, ct_ref, ccol_ref):
    tm = x_ref.shape[0]

    @pl.when(pl.program_id(1) == 0)
    def _():
        ccol_ref[...] = jnp.zeros_like(ccol_ref)

    x = x_ref[...]
    xn = x * lax.rsqrt(jnp.mean(x * x, axis=-1, keepdims=True) + EPS)
    h1 = (xn * gm_ref[...]).astype(BF16)
    h2 = (xn * gkv_ref[...]).astype(BF16)

    def head_norm(t, g):
        t2 = (t * t).astype(BF16)
        blk = 2 * LANES
        ss = jnp.concatenate(
            [_dot(t2[:, i * blk:(i + 1) * blk], ones_ref[...]) for i in range(D_MODEL // blk)], axis=1)
        return t * lax.rsqrt(ss * (1.0 / FOX_HD) + EPS) * g

    q = _dot(h1, wq_ref[...])
    q_ref[...] = (head_norm(q, gq_ref[...]) * (FOX_HD ** -0.5 * LOG2E)).astype(BF16)
    sg_ref[...] = _sigmoid(_dot(h1, wog_ref[...])).astype(BF16)
    k = _dot(h2, wk_ref[...])
    k_ref[...] = head_norm(k, gkn_ref[...]).astype(BF16)
    vft = _dot_nt(wvft_ref[...], h2)
    vt_ref[...] = vft[0:D_MODEL].astype(BF16)

    lf = _log_sigmoid(vft[D_MODEL:] + bf_ref[:, 0:1])
    hi, mid, lo = _split3(lf)
    triu = triu_ref[...]
    ct = _dot(hi, triu) + _dot(mid, triu) + _dot(lo, triu) + ccol_ref[:, 0:1]
    ccol_ref[...] = jnp.broadcast_to(ct[:, tm - 1:tm], ccol_ref.shape)
    ct2 = ct * LOG2E
    ct_ref[...] = ct2
    c2 = jnp.concatenate([ct2, jnp.zeros((LANES - FOX_HEADS, tm), F32)], axis=0).T
    hi = c2.astype(BF16).astype(F32)
    r = c2 - hi
    mid = r.astype(BF16).astype(F32)
    lo = r - mid
    lane = lax.broadcasted_iota(jnp.int32, c2.shape, 1)
    cs_ref[...] = jnp.where(
        lane < FOX_HEADS, hi,
        jnp.where(lane < 2 * FOX_HEADS, pltpu.roll(mid, FOX_HEADS, 1),
                  pltpu.roll(lo, 2 * FOX_HEADS, 1))).astype(BF16)


def _fox_proj(x2d, g_mix, g_kv, w_fox_in, w_kv_in, b_f, g_qnorm, g_knorm, batch, seq):
    tm = TM_PROJ
    nt = seq // tm
    d = D_MODEL
    wq = w_fox_in[:, :d].astype(BF16)
    wog = w_fox_in[:, d:].astype(BF16)
    wk = w_kv_in[:, :d].astype(BF16)
    wvft = w_kv_in[:, d:].T.astype(BF16)
    bf_col = jnp.broadcast_to(b_f[:, None], (FOX_HEADS, LANES))
    gq = jnp.tile(g_qnorm, FOX_HEADS).reshape(1, d)
    gkn = jnp.tile(g_knorm, FOX_HEADS).reshape(1, d)
    blk = 2 * LANES
    hid = jnp.arange(blk) // FOX_HD
    ones_bd = (hid[:, None] == hid[None, :]).astype(BF16)
    tid = jnp.arange(tm)
    triu = (tid[:, None] <= tid[None, :]).astype(BF16)
    row = lambda b, s: (b * nt + s, 0)
    return pl.pallas_call(
        _fox_proj_kernel,
        grid=(batch, nt),
        in_specs=[
            pl.BlockSpec((tm, d), row),
            _const_spec((1, d)), _const_spec((1, d)),
            _const_spec((d, d)), _const_spec((d, d)), _const_spec((d, d)),
            _const_spec((d + FOX_HEADS, d)), _const_spec((FOX_HEADS, LANES)),
            _const_spec((1, d)), _const_spec((1, d)),
            _const_spec((blk, blk)), _const_spec((tm, tm)),
        ],
        out_specs=[
            pl.BlockSpec((tm, d), row),
            pl.BlockSpec((tm, d), row),
            pl.BlockSpec((tm, d), row),
            pl.BlockSpec((d, tm), lambda b, s: (b, s)),
            pl.BlockSpec((tm, LANES), row),
            pl.BlockSpec((FOX_HEADS, tm), lambda b, s: (b, s)),
        ],
        out_shape=[
            jax.ShapeDtypeStruct((batch * seq, d), BF16),
            jax.ShapeDtypeStruct((batch * seq, d), BF16),
            jax.ShapeDtypeStruct((batch * seq, d), BF16),
            jax.ShapeDtypeStruct((batch * d, seq), BF16),
            jax.ShapeDtypeStruct((batch * seq, LANES), BF16),
            jax.ShapeDtypeStruct((batch * FOX_HEADS, seq), F32),
        ],
        scratch_shapes=[pltpu.VMEM((FOX_HEADS, LANES), F32)],
        compiler_params=_params(("parallel", "arbitrary")),
        name="fox_proj",
    )(x2d, g_mix.reshape(1, d), g_kv.reshape(1, d), wq, wog, wk, wvft, bf_col, gq, gkn,
      ones_bd, triu)


def _fox_attn_kernel(q_ref, k_ref, vt_ref, cs_ref, ct_ref, sg_ref, out_ref, s_ref, kaug_ref):
    seq = q_ref.shape[0]
    pair = pl.program_id(1)
    lane = lax.broadcasted_iota(jnp.int32, (TQ, LANES), 1)
    low = lane < FOX_HD
    kv_row = lax.broadcasted_iota(jnp.int32, (TK, TQ), 0)
    q_col = lax.broadcasted_iota(jnp.int32, (TK, TQ), 1)
    causal_t = jnp.concatenate([kv_row <= q_col] * 2, axis=1)

    src = lax.broadcasted_iota(jnp.int32, (LANES, LANES), 0)
    dst = lax.broadcasted_iota(jnp.int32, (LANES, LANES), 1)
    sel = jnp.zeros((LANES, LANES), F32)
    for hh in range(2):
        for term in range(3):
            hit = (dst == 3 * hh + term) & (src == FOX_HEADS * term + 2 * pair + hh)
            sel = jnp.where(hit, -1.0, sel)
    kaug_ref[:, 0:LANES] = k_ref[...]
    kaug_ref[:, LANES:] = _dot(cs_ref[...], sel.astype(BF16)).astype(BF16)
    q_ext = jnp.concatenate([jnp.where(lane < 3, 1.0, 0.0),
                             jnp.where((lane >= 3) & (lane < 6), 1.0, 0.0)], axis=0).astype(BF16)

    def pass1(qi):
        rows = slice(qi * TQ, (qi + 1) * TQ)
        qp = q_ref[rows, :]
        zero = jnp.zeros_like(qp)
        q_aug = jnp.concatenate(
            [jnp.concatenate([jnp.where(low, qp, zero), jnp.where(low, zero, qp)], axis=0), q_ext],
            axis=1)
        st_all = _dot_nt(kaug_ref[0:(qi + 1) * TK, :], q_aug)
        sbuf = s_ref.at[qi % 2]
        mrun = None
        for kb in range(qi + 1):
            krows = slice(kb * TK, (kb + 1) * TK)
            st = st_all[krows]
            if kb == qi:
                st = jnp.where(causal_t, st, -jnp.inf)
            sbuf[krows, :] = st
            bm = jnp.max(st.reshape(TK // SUBLANES, SUBLANES, 2 * TQ), axis=0)
            mrun = bm if mrun is None else jnp.maximum(mrun, bm)
            yield None
        cq = jnp.concatenate([ct_ref[pl.ds(2 * pair + hh, 1), rows] for hh in range(2)], axis=1)
        logit_max = jnp.max(mrun, axis=0, keepdims=True) + cq
        yield logit_max - cq

    def pass2(qi, shift):
        rows = slice(qi * TQ, (qi + 1) * TQ)
        sbuf = s_ref.at[qi % 2]
        acc = None
        lrun = None
        for kb in range(qi + 1):
            krows = slice(kb * TK, (kb + 1) * TK)
            pt = jnp.exp2(sbuf[krows, :] - shift)
            ps = jnp.sum(pt.reshape(TK // SUBLANES, SUBLANES, 2 * TQ), axis=0)
            lrun = ps if lrun is None else lrun + ps
            d = _dot(vt_ref[:, krows], pt.astype(BF16))
            acc = d if acc is None else acc + d
            yield None
        ot = acc / jnp.sum(lrun, axis=0, keepdims=True)
        o = jnp.concatenate([ot[:FOX_HD, :TQ], ot[FOX_HD:, TQ:]], axis=0).T
        out_ref[rows, :] = (o * sg_ref[rows, :].astype(F32)).astype(BF16)
        yield None

    def interleave(gen_a, gen_b):
        last_a = None
        while gen_a is not None or gen_b is not None:
            if gen_a is not None:
                try:
                    last_a = next(gen_a)
                except StopIteration:
                    gen_a = None
            if gen_b is not None:
                try:
                    next(gen_b)
                except StopIteration:
                    gen_b = None
        return last_a

    nq = seq // TQ
    shift = interleave(pass1(0), None)
    for qi in range(nq):
        shift = interleave(pass1(qi + 1) if qi + 1 < nq else None, pass2(qi, shift))


def _fox_attn(q, k, vt, cs, ct, sg, batch, seq):
    d = D_MODEL
    pairs = FOX_HEADS // 2
    blk = lambda b, j: (b, j)
    return pl.pallas_call(
        _fox_attn_kernel,
        grid=(batch, pairs),
        in_specs=[
            pl.BlockSpec((seq, LANES), blk),
            pl.BlockSpec((seq, LANES), blk),
            pl.BlockSpec((LANES, seq), lambda b, j: (b * pairs + j, 0)),
            pl.BlockSpec((seq, LANES), lambda b, j: (b, 0)),
            pl.BlockSpec((FOX_HEADS, seq), lambda b, j: (b, 0)),
            pl.BlockSpec((seq, LANES), blk),
        ],
        out_specs=pl.BlockSpec((seq, LANES), blk),
        out_shape=jax.ShapeDtypeStruct((batch * seq, d), BF16),
        scratch_shapes=[pltpu.VMEM((2, seq, 2 * TQ), F32), pltpu.VMEM((seq, 2 * LANES), BF16)],
        compiler_params=_params(("parallel", "parallel")),
        name="fox_attn",
    )(q, k, vt, cs, ct, sg)


def kernel(x, p, g_mix, g_ffn, g_ple, gla_w_in, gla_w_gk2, gla_b_gk2, gla_g_onorm, gla_w_out,
           kv_g_norm, kv_w_in, kv_g_knorm, kv_b_f, fox_w_in, fox_g_qnorm, fox_w_out, ffn_w_up,
           ffn_w_dw, ffn_b_dw, ffn_w_down, ple_w_gate, ple_b_gate, ple_w_proj, g_final):
    batch, seq, d = x.shape
    depth = p.shape[0]
    n_gla = gla_w_in.shape[0]
    x2d = x.reshape(batch * seq, d)
    p3d = p.reshape(depth, batch * seq, PLE_DIM)
    for i in range(depth):
        if i < n_gla:
            mix = _gla_mixer(x2d, g_mix[i], gla_w_in[i], gla_w_gk2[i], gla_b_gk2[i],
                             gla_g_onorm[i], batch, seq)
            w_mix = gla_w_out[i]
        else:
            j = i - n_gla
            assert j == 0, "a single FoX layer is supported"
            q, sg, k_sh, vt_sh, cs_sh, ct_sh = _fox_proj(
                x2d, g_mix[i], kv_g_norm, fox_w_in[j], kv_w_in, kv_b_f, fox_g_qnorm[j],
                kv_g_knorm, batch, seq)
            mix = _fox_attn(q, k_sh, vt_sh, cs_sh, ct_sh, sg, batch, seq)
            w_mix = fox_w_out[j]
        x2d = _ffn_ple(x2d, mix, w_mix, p3d, i, g_ffn[i], ffn_w_up[i], ffn_w_dw[i], ffn_b_dw[i],
                       ffn_w_down[i], g_ple[i], ple_w_gate[i], ple_b_gate[i], ple_w_proj[i],
                       g_final, batch, seq, final_norm=(i == depth - 1))
    return x2d.reshape(batch, seq, d)
```

```python
import functools
import math

import jax
import jax.numpy as jnp
from jax import lax
from jax.experimental import pallas as pl
from jax.experimental.pallas import tpu as pltpu

F32 = jnp.float32
BF16 = jnp.bfloat16

EPS = 1e-6
D_MODEL = 1024
GLA_HEADS = 4
GLA_DK = 128
GLA_DV = 256
GLA_QK = GLA_HEADS * GLA_DK
GLA_V = GLA_HEADS * GLA_DV
GLA_RANK = 16
GLA_TAU = 16.0
GLA_CHUNK = 64
FOX_HEADS = 16
FOX_HD = 64
D_FF = 2816
PLE_DIM = 256
LOG2E = math.log2(math.e)

LANES = 128
SUBLANES = 8
VMEM_LIMIT = 56 * 1024 * 1024

TM_PROJ = 1024
TM_GLA = 1024
TM_FFN = 512
FFN_CHUNK = 256
FFN_PITCH = 72
TQ = 256
TK = 256


def _dot(a, b):
    return jnp.dot(a, b, preferred_element_type=F32)


def _dot_nt(a, b):
    return lax.dot_general(a, b, (((1,), (1,)), ((), ())), preferred_element_type=F32)


def _dot_tn(a, b):
    return lax.dot_general(a, b, (((0,), (0,)), ((), ())), preferred_element_type=F32)


def _split3(x):
    hi = x.astype(BF16)
    r = x - hi.astype(F32)
    mid = r.astype(BF16)
    lo = (r - mid.astype(F32)).astype(BF16)
    return hi, mid, lo


def _rms(x, g):
    return x * lax.rsqrt(jnp.mean(x * x, axis=-1, keepdims=True) + EPS) * g


def _sigmoid(z):
    return 1.0 / (1.0 + jnp.exp(-z))


def _log_sigmoid(z):
    return jnp.minimum(z, 0.0) - jnp.log1p(jnp.exp(-jnp.abs(z)))


def _const_spec(shape):
    nd = len(shape)
    return pl.BlockSpec(shape, lambda *_: (0,) * nd, pipeline_mode=pl.Buffered(1))


def _params(sem):
    return pltpu.CompilerParams(dimension_semantics=sem, vmem_limit_bytes=VMEM_LIMIT)


def _gla_kernel(x_ref, g_ref, w_ref, wg2_ref, bg2_ref, gon_ref, out_ref, state_ref, o_ref):
    c = GLA_CHUNK
    tm = x_ref.shape[0]

    @pl.when(pl.program_id(1) == 0)
    def _():
        state_ref[...] = jnp.zeros_like(state_ref)

    hn = _rms(x_ref[...], g_ref[...]).astype(BF16)
    q = _dot(hn, w_ref[:, 0:GLA_QK]).astype(BF16)
    k = _dot(hn, w_ref[:, GLA_QK:2 * GLA_QK]).astype(BF16)
    v = _dot(hn, w_ref[:, 2 * GLA_QK:2 * GLA_QK + GLA_V]).astype(BF16)
    og = _dot(hn, w_ref[:, 2 * GLA_QK + GLA_V:2 * GLA_QK + 2 * GLA_V]).astype(BF16)
    gr = _dot(hn, w_ref[:, 2 * GLA_QK + 2 * GLA_V:]).astype(BF16)
    gk = _log_sigmoid(_dot(gr, wg2_ref[...]) + bg2_ref[...]) * (1.0 / GLA_TAU)

    rowi = lax.broadcasted_iota(jnp.int32, (c, c), 0)
    coli = lax.broadcasted_iota(jnp.int32, (c, c), 1)
    tril = rowi >= coli
    tril_bf = jnp.where(tril, 1.0, 0.0).astype(BF16)
    scale = GLA_DK ** -0.5

    nchunk = tm // c
    chunk_rows = [slice(ci * c, (ci + 1) * c) for ci in range(nchunk)]
    head_k = [slice(h * GLA_DK, (h + 1) * GLA_DK) for h in range(GLA_HEADS)]
    head_v = [slice(h * GLA_DV, (h + 1) * GLA_DV) for h in range(GLA_HEADS)]

    hi, mid, lo = _split3(gk)
    bcum = jnp.concatenate(
        [_dot(tril_bf, hi[r]) + _dot(tril_bf, mid[r]) + _dot(tril_bf, lo[r]) for r in chunk_rows],
        axis=0)
    b_last = [bcum[r.stop - 1:r.stop, :] for r in chunk_rows]
    b_last_rows = jnp.concatenate([jnp.broadcast_to(b, (c, GLA_QK)) for b in b_last], axis=0)
    qf = q.astype(F32) * scale
    kf = k.astype(F32)
    q_in = (qf * jnp.exp(bcum)).astype(BF16)
    k_in = (kf * jnp.exp(-bcum)).astype(BF16)
    k_end = (kf * jnp.exp(b_last_rows - bcum)).astype(BF16)
    decay = jnp.exp(jnp.concatenate(b_last, axis=0))
    pad = jnp.zeros((LANES - nchunk, GLA_DK), F32)
    decay_t = [jnp.concatenate([decay[:, ks], pad], axis=0).T for ks in head_k]

    att = [[jnp.where(tril, _dot_nt(q_in[r, ks], k_in[r, ks]), 0.0).astype(BF16) for ks in head_k]
           for r in chunk_rows]
    upd = [[_dot_tn(k_end[r, head_k[h]], v[r, head_v[h]]) for h in range(GLA_HEADS)]
           for r in chunk_rows]

    state = [state_ref[h] for h in range(GLA_HEADS)]
    for ci, r in enumerate(chunk_rows):
        for h in range(GLA_HEADS):
            o_ref[r, head_v[h]] = (_dot(att[ci][h], v[r, head_v[h]])
                                   + _dot(q_in[r, head_k[h]], state[h].astype(BF16)))
            state[h] = decay_t[h][:, ci:ci + 1] * state[h] + upd[ci][h]
    for h in range(GLA_HEADS):
        state_ref[h] = state[h]

    for h in range(GLA_HEADS):
        vs = head_v[h]
        on = _rms(o_ref[:, vs], gon_ref[...])
        og_h = og[:, vs].astype(F32)
        out_ref[:, vs] = (on * (og_h * _sigmoid(og_h))).astype(BF16)


def _gla_mixer(x2d, g, w_in, w_gk2, b_gk2, g_onorm, batch, seq):
    tm = TM_GLA
    nt = seq // tm
    n_main = 2 * GLA_QK + 2 * GLA_V
    w_pad = jnp.zeros((D_MODEL, n_main + LANES), BF16).at[:, :n_main + GLA_RANK].set(w_in.astype(BF16))
    wg2_pad = jnp.zeros((LANES, GLA_QK), BF16).at[:GLA_RANK].set(w_gk2.astype(BF16))
    row = lambda b, s: (b * nt + s, 0)
    return pl.pallas_call(
        _gla_kernel,
        grid=(batch, nt),
        in_specs=[
            pl.BlockSpec((tm, D_MODEL), row),
            _const_spec((1, D_MODEL)),
            _const_spec((D_MODEL, n_main + LANES)),
            _const_spec((LANES, GLA_QK)),
            _const_spec((1, GLA_QK)),
            _const_spec((1, GLA_DV)),
        ],
        out_specs=pl.BlockSpec((tm, GLA_V), row),
        out_shape=jax.ShapeDtypeStruct((batch * seq, GLA_V), BF16),
        scratch_shapes=[
            pltpu.VMEM((GLA_HEADS, GLA_DK, GLA_DV), F32),
            pltpu.VMEM((tm, GLA_V), F32),
        ],
        compiler_params=_params(("parallel", "arbitrary")),
        name="gla_mixer",
    )(x2d, g.reshape(1, D_MODEL), w_pad, wg2_pad, b_gk2.reshape(1, GLA_QK),
      g_onorm.reshape(1, GLA_DV))


def _ffn_kernel(x_ref, mix_ref, wmix_ref, p_ref, gf_ref, wup_ref, wdw_ref, bdw_ref, wdn_ref, gp_ref,
                wg_ref, bg_ref, wp_ref, gfin_ref, out_ref, carry_ref, act_ref, perm_ref, *, final_norm):
    tm = x_ref.shape[0]
    fc = FFN_CHUNK
    nch = D_FF // fc
    sub = SUBLANES
    span = tm // sub
    nlb = D_MODEL // LANES

    @pl.when(pl.program_id(1) == 0)
    def _():
        carry_ref[...] = jnp.zeros_like(carry_ref)

    def interleave(val):
        for lb in range(nlb):
            for s in range(sub):
                perm_ref[lb, s * FFN_PITCH:s * FFN_PITCH + span, :] = (
                    val[s * span:(s + 1) * span, lb * LANES:(lb + 1) * LANES])
        return jnp.concatenate(
            [jnp.concatenate([perm_ref[lb, pl.ds(v, sub, stride=FFN_PITCH), :] for lb in range(nlb)],
                             axis=1) for v in range(span)], axis=0)

    def deinterleave(val):
        for lb in range(nlb):
            for v in range(span):
                perm_ref[lb, pl.ds(v, sub, stride=FFN_PITCH), :] = (
                    val[v * sub:(v + 1) * sub, lb * LANES:(lb + 1) * LANES])
        return jnp.concatenate(
            [jnp.concatenate([perm_ref[lb, s * FFN_PITCH:s * FFN_PITCH + span, :] for lb in range(nlb)],
                             axis=1) for s in range(sub)], axis=0)

    x = x_ref[...] + _dot(mix_ref[...], wmix_ref[...])
    hn = interleave(_rms(x, gf_ref[...])).astype(BF16)
    rowid = lax.broadcasted_iota(jnp.int32, (sub, fc), 0)

    def wrap(cur, prev):
        return jnp.where(rowid == 0, pltpu.roll(prev, 1, 0), pltpu.roll(cur, 1, 0))

    def conv(col0, slot):
        u = _dot(hn, wup_ref[:, col0:col0 + fc])
        prev = carry_ref[slot]
        carry_ref[slot] = u[tm - 2 * sub:tm, :]
        top1 = wrap(u[tm - sub:tm, :], prev[sub:2 * sub, :])
        top2 = wrap(u[tm - 2 * sub:tm - sub, :], prev[0:sub, :])
        u1 = jnp.concatenate([top1, u[0:tm - sub, :]], axis=0)
        u2 = jnp.concatenate([top2, top1, u[0:tm - 2 * sub, :]], axis=0)
        w = wdw_ref[:, col0:col0 + fc]
        return w[0:1] * u2 + w[1:2] * u1 + w[2:3] * u + bdw_ref[:, col0:col0 + fc]

    for ci in range(nch):
        a = conv(ci * fc, 2 * ci)
        b = conv(D_FF + ci * fc, 2 * ci + 1)
        act_ref[:, ci * fc:(ci + 1) * fc] = (a * _sigmoid(a) * b).astype(BF16)
    y = x + deinterleave(_dot(act_ref[...], wdn_ref[...]))
    hp = _rms(y, gp_ref[...]).astype(BF16)
    gate = _sigmoid(_dot(hp, wg_ref[...]) + bg_ref[...])
    pe = _dot(p_ref[...].astype(BF16), wp_ref[...])
    y = y + pe * gate
    if final_norm:
        y = _rms(y, gfin_ref[...])
    out_ref[...] = y


def _ffn_ple(x2d, mix, w_mix, p3d, layer, g_ffn, w_up, w_dw, b_dw, w_down, g_ple, w_gate, b_gate,
             w_proj, g_final, batch, seq, final_norm):
    tm = TM_FFN
    nt = seq // tm
    nslots = 2 * (D_FF // FFN_CHUNK)
    row = lambda b, s: (b * nt + s, 0)
    return pl.pallas_call(
        functools.partial(_ffn_kernel, final_norm=final_norm),
        grid=(batch, nt),
        in_specs=[
            pl.BlockSpec((tm, D_MODEL), row),
            pl.BlockSpec((tm, D_MODEL), row),
            _const_spec((D_MODEL, D_MODEL)),
            pl.BlockSpec((None, tm, PLE_DIM), lambda b, s: (layer, b * nt + s, 0)),
            _const_spec((1, D_MODEL)),
            _const_spec((D_MODEL, 2 * D_FF)),
            _const_spec((3, 2 * D_FF)),
            _const_spec((1, 2 * D_FF)),
            _const_spec((D_FF, D_MODEL)),
            _const_spec((1, D_MODEL)),
            _const_spec((D_MODEL, D_MODEL)),
            _const_spec((1, D_MODEL)),
            _const_spec((PLE_DIM, D_MODEL)),
            _const_spec((1, D_MODEL)),
        ],
        out_specs=pl.BlockSpec((tm, D_MODEL), row),
        out_shape=jax.ShapeDtypeStruct(x2d.shape, F32),
        scratch_shapes=[
            pltpu.VMEM((nslots, 2 * SUBLANES, FFN_CHUNK), F32),
            pltpu.VMEM((tm, D_FF), BF16),
            pltpu.VMEM((D_MODEL // LANES, SUBLANES * FFN_PITCH, LANES), F32),
        ],
        compiler_params=_params(("parallel", "arbitrary")),
        name="ffn_ple",
    )(x2d, mix, w_mix.astype(BF16), p3d, g_ffn.reshape(1, -1), w_up.astype(BF16), w_dw,
      b_dw.reshape(1, -1), w_down.astype(BF16), g_ple.reshape(1, -1), w_gate.astype(BF16),
      b_gate.reshape(1, -1), w_proj.astype(BF16), g_final.reshape(1, -1))


def _fox_proj_kernel(x_ref, gm_ref, gkv_ref, wq_ref, wog_ref, wk_ref, wvft_ref, bf_ref,
                     gq_ref, gkn_ref, ones_ref, triu_ref,
                     q_ref, sg_ref, k_ref, vt_ref, cs_ref, ct_ref, ccol_ref):
    tm = x_ref.shape[0]

    @pl.when(pl.program_id(1) == 0)
    def _():
        ccol_ref[...] = jnp.zeros_like(ccol_ref)

    x = x_ref[...]
    xn = x * lax.rsqrt(jnp.mean(x * x, axis=-1, keepdims=True) + EPS)
    h1 = (xn * gm_ref[...]).astype(BF16)
    h2 = (xn * gkv_ref[...]).astype(BF16)

    def head_norm(t, g):
        t2 = (t * t).astype(BF16)
        blk = 2 * LANES
        ss = jnp.concatenate(
            [_dot(t2[:, i * blk:(i + 1) * blk], ones_ref[...]) for i in range(D_MODEL // blk)], axis=1)
        return t * lax.rsqrt(ss * (1.0 / FOX_HD) + EPS) * g

    q = _dot(h1, wq_ref[...])
    q_ref[...] = (head_norm(q, gq_ref[...]) * (FOX_HD ** -0.5 * LOG2E)).astype(BF16)
    sg_ref[...] = _sigmoid(_dot(h1, wog_ref[...])).astype(BF16)
    k = _dot(h2, wk_ref[...])
    k_ref[...] = head_norm(k, gkn_ref[...]).astype(BF16)
    vft = _dot_nt(wvft_ref[...], h2)
    vt_ref[...] = vft[0:D_MODEL].astype(BF16)

    lf = _log_sigmoid(vft[D_MODEL:] + bf_ref[:, 0:1])
    hi, mid, lo = _split3(lf)
    triu = triu_ref[...]
    ct = _dot(hi, triu) + _dot(mid, triu) + _dot(lo, triu) + ccol_ref[:, 0:1]
    ccol_ref[...] = jnp.broadcast_to(ct[:, tm - 1:tm], ccol_ref.shape)
    ct2 = ct * LOG2E
    ct_ref[...] = ct2
    c2 = jnp.concatenate([ct2, jnp.zeros((LANES - FOX_HEADS, tm), F32)], axis=0).T
    hi = c2.astype(BF16).astype(F32)
    r = c2 - hi
    mid = r.astype(BF16).astype(F32)
    lo = r - mid
    lane = lax.broadcasted_iota(jnp.int32, c2.shape, 1)
    cs_ref[...] = jnp.where(
        lane < FOX_HEADS, hi,
        jnp.where(lane < 2 * FOX_HEADS, pltpu.roll(mid, FOX_HEADS, 1),
                  pltpu.roll(lo, 2 * FOX_HEADS, 1))).astype(BF16)


def _fox_proj(x2d, g_mix, g_kv, w_fox_in, w_kv_in, b_f, g_qnorm, g_knorm, batch, seq):
    tm = TM_PROJ
    nt = seq // tm
    d = D_MODEL
    wq = w_fox_in[:, :d].astype(BF16)
    wog = w_fox_in[:, d:].astype(BF16)
    wk = w_kv_in[:, :d].astype(BF16)
    wvft = w_kv_in[:, d:].T.astype(BF16)
    bf_col = jnp.broadcast_to(b_f[:, None], (FOX_HEADS, LANES))
    gq = jnp.tile(g_qnorm, FOX_HEADS).reshape(1, d)
    gkn = jnp.tile(g_knorm, FOX_HEADS).reshape(1, d)
    blk = 2 * LANES
    hid = jnp.arange(blk) // FOX_HD
    ones_bd = (hid[:, None] == hid[None, :]).astype(BF16)
    tid = jnp.arange(tm)
    triu = (tid[:, None] <= tid[None, :]).astype(BF16)
    row = lambda b, s: (b * nt + s, 0)
    return pl.pallas_call(
        _fox_proj_kernel,
        grid=(batch, nt),
        in_specs=[
            pl.BlockSpec((tm, d), row),
            _const_spec((1, d)), _const_spec((1, d)),
            _const_spec((d, d)), _const_spec((d, d)), _const_spec((d, d)),
            _const_spec((d + FOX_HEADS, d)), _const_spec((FOX_HEADS, LANES)),
            _const_spec((1, d)), _const_spec((1, d)),
            _const_spec((blk, blk)), _const_spec((tm, tm)),
        ],
        out_specs=[
            pl.BlockSpec((tm, d), row),
            pl.BlockSpec((tm, d), row),
            pl.BlockSpec((tm, d), row),
            pl.BlockSpec((d, tm), lambda b, s: (b, s)),
            pl.BlockSpec((tm, LANES), row),
            pl.BlockSpec((FOX_HEADS, tm), lambda b, s: (b, s)),
        ],
        out_shape=[
            jax.ShapeDtypeStruct((batch * seq, d), BF16),
            jax.ShapeDtypeStruct((batch * seq, d), BF16),
            jax.ShapeDtypeStruct((batch * seq, d), BF16),
            jax.ShapeDtypeStruct((batch * d, seq), BF16),
            jax.ShapeDtypeStruct((batch * seq, LANES), BF16),
            jax.ShapeDtypeStruct((batch * FOX_HEADS, seq), F32),
        ],
        scratch_shapes=[pltpu.VMEM((FOX_HEADS, LANES), F32)],
        compiler_params=_params(("parallel", "arbitrary")),
        name="fox_proj",
    )(x2d, g_mix.reshape(1, d), g_kv.reshape(1, d), wq, wog, wk, wvft, bf_col, gq, gkn,
      ones_bd, triu)


def _fox_attn_kernel(q_ref, k_ref, vt_ref, cs_ref, ct_ref, sg_ref, out_ref, s_ref, kaug_ref):
    seq = q_ref.shape[0]
    pair = pl.program_id(1)
    lane = lax.broadcasted_iota(jnp.int32, (TQ, LANES), 1)
    low = lane < FOX_HD
    kv_row = lax.broadcasted_iota(jnp.int32, (TK, TQ), 0)
    q_col = lax.broadcasted_iota(jnp.int32, (TK, TQ), 1)
    causal_t = jnp.concatenate([kv_row <= q_col] * 2, axis=1)

    src = lax.broadcasted_iota(jnp.int32, (LANES, LANES), 0)
    dst = lax.broadcasted_iota(jnp.int32, (LANES, LANES), 1)
    sel = jnp.zeros((LANES, LANES), F32)
    for hh in range(2):
        for term in range(3):
            hit = (dst == 3 * hh + term) & (src == FOX_HEADS * term + 2 * pair + hh)
            sel = jnp.where(hit, -1.0, sel)
    kaug_ref[:, 0:LANES] = k_ref[...]
    kaug_ref[:, LANES:] = _dot(cs_ref[...], sel.astype(BF16)).astype(BF16)
    q_ext = jnp.concatenate([jnp.where(lane < 3, 1.0, 0.0),
                             jnp.where((lane >= 3) & (lane < 6), 1.0, 0.0)], axis=0).astype(BF16)

    def pass1(qi):
        rows = slice(qi * TQ, (qi + 1) * TQ)
        qp = q_ref[rows, :]
        zero = jnp.zeros_like(qp)
        q_aug = jnp.concatenate(
            [jnp.concatenate([jnp.where(low, qp, zero), jnp.where(low, zero, qp)], axis=0), q_ext],
            axis=1)
        st_all = _dot_nt(kaug_ref[0:(qi + 1) * TK, :], q_aug)
        sbuf = s_ref.at[qi % 2]
        mrun = None
        for kb in range(qi + 1):
            krows = slice(kb * TK, (kb + 1) * TK)
            st = st_all[krows]
            if kb == qi:
                st = jnp.where(causal_t, st, -jnp.inf)
            sbuf[krows, :] = st
            bm = jnp.max(st.reshape(TK // SUBLANES, SUBLANES, 2 * TQ), axis=0)
            mrun = bm if mrun is None else jnp.maximum(mrun, bm)
            yield None
        cq = jnp.concatenate([ct_ref[pl.ds(2 * pair + hh, 1), rows] for hh in range(2)], axis=1)
        logit_max = jnp.max(mrun, axis=0, keepdims=True) + cq
        yield logit_max - cq

    def pass2(qi, shift):
        rows = slice(qi * TQ, (qi + 1) * TQ)
        sbuf = s_ref.at[qi % 2]
        acc = None
        lrun = None
        for kb in range(qi + 1):
            krows = slice(kb * TK, (kb + 1) * TK)
            pt = jnp.exp2(sbuf[krows, :] - shift)
            ps = jnp.sum(pt.reshape(TK // SUBLANES, SUBLANES, 2 * TQ), axis=0)
            lrun = ps if lrun is None else lrun + ps
            d = _dot(vt_ref[:, krows], pt.astype(BF16))
            acc = d if acc is None else acc + d
            yield None
        ot = acc / jnp.sum(lrun, axis=0, keepdims=True)
        o = jnp.concatenate([ot[:FOX_HD, :TQ], ot[FOX_HD:, TQ:]], axis=0).T
        out_ref[rows, :] = (o * sg_ref[rows, :].astype(F32)).astype(BF16)
        yield None

    def interleave(gen_a, gen_b):
        last_a = None
        while gen_a is not None or gen_b is not None:
            if gen_a is not None:
                try:
                    last_a = next(gen_a)
                except StopIteration:
                    gen_a = None
            if gen_b is not None:
                try:
                    next(gen_b)
                except StopIteration:
                    gen_b = None
        return last_a

    nq = seq // TQ
    shift = interleave(pass1(0), None)
    for qi in range(nq):
        shift = interleave(pass1(qi + 1) if qi + 1 < nq else None, pass2(qi, shift))


def _fox_attn(q, k, vt, cs, ct, sg, batch, seq):
    d = D_MODEL
    pairs = FOX_HEADS // 2
    blk = lambda b, j: (b, j)
    return pl.pallas_call(
        _fox_attn_kernel,
        grid=(batch, pairs),
        in_specs=[
            pl.BlockSpec((seq, LANES), blk),
            pl.BlockSpec((seq, LANES), blk),
            pl.BlockSpec((LANES, seq), lambda b, j: (b * pairs + j, 0)),
            pl.BlockSpec((seq, LANES), lambda b, j: (b, 0)),
            pl.BlockSpec((FOX_HEADS, seq), lambda b, j: (b, 0)),
            pl.BlockSpec((seq, LANES), blk),
        ],
        out_specs=pl.BlockSpec((seq, LANES), blk),
        out_shape=jax.ShapeDtypeStruct((batch * seq, d), BF16),
        scratch_shapes=[pltpu.VMEM((2, seq, 2 * TQ), F32), pltpu.VMEM((seq, 2 * LANES), BF16)],
        compiler_params=_params(("parallel", "parallel")),
        name="fox_attn",
    )(q, k, vt, cs, ct, sg)


def kernel(x, p, g_mix, g_ffn, g_ple, gla_w_in, gla_w_gk2, gla_b_gk2, gla_g_onorm, gla_w_out,
           kv_g_norm, kv_w_in, kv_g_knorm, kv_b_f, fox_w_in, fox_g_qnorm, fox_w_out, ffn_w_up,
           ffn_w_dw, ffn_b_dw, ffn_w_down, ple_w_gate, ple_b_gate, ple_w_proj, g_final):
    batch, seq, d = x.shape
    depth = p.shape[0]
    n_gla = gla_w_in.shape[0]
    x2d = x.reshape(batch * seq, d)
    p3d = p.reshape(depth, batch * seq, PLE_DIM)
    for i in range(depth):
        if i < n_gla:
            mix = _gla_mixer(x2d, g_mix[i], gla_w_in[i], gla_w_gk2[i], gla_b_gk2[i],
                             gla_g_onorm[i], batch, seq)
            w_mix = gla_w_out[i]
        else:
            j = i - n_gla
            assert j == 0, "a single FoX layer is supported"
            q, sg, k_sh, vt_sh, cs_sh, ct_sh = _fox_proj(
                x2d, g_mix[i], kv_g_norm, fox_w_in[j], kv_w_in, kv_b_f, fox_g_qnorm[j],
                kv_g_knorm, batch, seq)
            mix = _fox_attn(q, k_sh, vt_sh, cs_sh, ct_sh, sg, batch, seq)
            w_mix = fox_w_out[j]
        x2d = _ffn_ple(x2d, mix, w_mix, p3d, i, g_ffn[i], ffn_w_up[i], ffn_w_dw[i], ffn_b_dw[i],
                       ffn_w_down[i], g_ple[i], ple_w_gate[i], ple_b_gate[i], ple_w_proj[i],
                       g_final, batch, seq, final_norm=(i == depth - 1))
    return x2d.reshape(batch, seq, d)
```

```python
import functools
import math

import jax
import jax.numpy as jnp
from jax import lax
from jax.experimental import pallas as pl
from jax.experimental.pallas import tpu as pltpu

F32 = jnp.float32
BF16 = jnp.bfloat16

EPS = 1e-6
D_MODEL = 1024
GLA_HEADS = 4
GLA_DK = 128
GLA_DV = 256
GLA_QK = GLA_HEADS * GLA_DK
GLA_V = GLA_HEADS * GLA_DV
GLA_RANK = 16
GLA_TAU = 16.0
GLA_CHUNK = 64
FOX_HEADS = 16
FOX_HD = 64
D_FF = 2816
PLE_DIM = 256
LOG2E = math.log2(math.e)

LANES = 128
SUBLANES = 8
VMEM_LIMIT = 56 * 1024 * 1024

TM_PROJ = 1024
TM_GLA = 1024
GLA_BLOCK = 512
TM_FFN = 512
FFN_CHUNK = 256
FFN_PITCH = 72
TQ = 256
TK = 256
ATT_AHEAD = 2


def _dot(a, b):
    return jnp.dot(a, b, preferred_element_type=F32)


def _dot_nt(a, b):
    return lax.dot_general(a, b, (((1,), (1,)), ((), ())), preferred_element_type=F32)


def _dot_tn(a, b):
    return lax.dot_general(a, b, (((0,), (0,)), ((), ())), preferred_element_type=F32)


def _split3(x):
    hi = x.astype(BF16)
    r = x - hi.astype(F32)
    mid = r.astype(BF16)
    lo = (r - mid.astype(F32)).astype(BF16)
    return hi, mid, lo


def _rms(x, g):
    return x * lax.rsqrt(jnp.mean(x * x, axis=-1, keepdims=True) + EPS) * g


def _sigmoid(z):
    return 1.0 / (1.0 + jnp.exp(-z))


def _log_sigmoid(z):
    return jnp.minimum(z, 0.0) - jnp.log1p(jnp.exp(-jnp.abs(z)))


def _const_spec(shape):
    nd = len(shape)
    return pl.BlockSpec(shape, lambda *_: (0,) * nd, pipeline_mode=pl.Buffered(1))


def _params(sem):
    return pltpu.CompilerParams(dimension_semantics=sem, vmem_limit_bytes=VMEM_LIMIT)


def _gla_kernel(x_ref, g_ref, w_ref, wg2_ref, bg2_ref, gon_ref, out_ref, state_ref, o_ref):
    c = GLA_CHUNK
    tm = x_ref.shape[0]
    blk = GLA_BLOCK

    @pl.when(pl.program_id(1) == 0)
    def _():
        state_ref[...] = jnp.zeros_like(state_ref)

    rowi = lax.broadcasted_iota(jnp.int32, (c, c), 0)
    coli = lax.broadcasted_iota(jnp.int32, (c, c), 1)
    tril = rowi >= coli
    tril_bf = jnp.where(tril, 1.0, 0.0).astype(BF16)
    scale = GLA_DK ** -0.5
    nchunk = blk // c
    chunk_rows = [slice(ci * c, (ci + 1) * c) for ci in range(nchunk)]
    head_k = [slice(h * GLA_DK, (h + 1) * GLA_DK) for h in range(GLA_HEADS)]
    head_v = [slice(h * GLA_DV, (h + 1) * GLA_DV) for h in range(GLA_HEADS)]

    def project(rows):
        hn = _rms(x_ref[rows, :], g_ref[...]).astype(BF16)
        yield
        q = _dot(hn, w_ref[:, 0:GLA_QK]).astype(BF16)
        yield
        k = _dot(hn, w_ref[:, GLA_QK:2 * GLA_QK]).astype(BF16)
        yield
        v = _dot(hn, w_ref[:, 2 * GLA_QK:2 * GLA_QK + GLA_V]).astype(BF16)
        yield
        og = _dot(hn, w_ref[:, 2 * GLA_QK + GLA_V:2 * GLA_QK + 2 * GLA_V]).astype(BF16)
        yield
        gr = _dot(hn, w_ref[:, 2 * GLA_QK + 2 * GLA_V:]).astype(BF16)
        gk = _log_sigmoid(_dot(gr, wg2_ref[...]) + bg2_ref[...]) * (1.0 / GLA_TAU)
        return q, k, v, og, gk

    def scan(proj, rows, state):
        q, k, v, og, gk = proj
        hi, mid, lo = _split3(gk)
        bcum = jnp.concatenate(
            [_dot(tril_bf, hi[r]) + _dot(tril_bf, mid[r]) + _dot(tril_bf, lo[r]) for r in chunk_rows],
            axis=0)
        yield
        b_last = [bcum[r.stop - 1:r.stop, :] for r in chunk_rows]
        b_last_rows = jnp.concatenate([jnp.broadcast_to(b, (c, GLA_QK)) for b in b_last], axis=0)
        qf = q.astype(F32) * scale
        kf = k.astype(F32)
        q_in = (qf * jnp.exp(bcum)).astype(BF16)
        yield
        k_in = (kf * jnp.exp(-bcum)).astype(BF16)
        yield
        k_end = (kf * jnp.exp(b_last_rows - bcum)).astype(BF16)
        decay = jnp.exp(jnp.concatenate(b_last, axis=0))
        pad = jnp.zeros((LANES - nchunk, GLA_DK), F32)
        decay_t = [jnp.concatenate([decay[:, ks], pad], axis=0).T for ks in head_k]
        yield
        att = [[jnp.where(tril, _dot_nt(q_in[r, ks], k_in[r, ks]), 0.0).astype(BF16)
                for ks in head_k] for r in chunk_rows]
        yield
        upd = [[_dot_tn(k_end[r, head_k[h]], v[r, head_v[h]]) for h in range(GLA_HEADS)]
               for r in chunk_rows]
        yield
        state = list(state)
        for ci, r in enumerate(chunk_rows):
            orow = slice(rows.start + r.start, rows.start + r.stop)
            for h in range(GLA_HEADS):
                o_ref[orow, head_v[h]] = (_dot(att[ci][h], v[r, head_v[h]])
                                          + _dot(q_in[r, head_k[h]], state[h].astype(BF16)))
                state[h] = decay_t[h][:, ci:ci + 1] * state[h] + upd[ci][h]
            yield
        for h in range(GLA_HEADS):
            vs = head_v[h]
            on = _rms(o_ref[rows, vs], gon_ref[...])
            og_h = og[:, vs].astype(F32)
            out_ref[rows, vs] = (on * (og_h * _sigmoid(og_h))).astype(BF16)
            yield
        return state

    def step(gen):
        try:
            next(gen)
            return False, None
        except StopIteration as stop:
            return True, stop.value

    def drive(main, side):
        side_done, side_val = (True, None) if side is None else (False, None)
        while True:
            done, val = step(main)
            if not side_done:
                side_done, side_val = step(side)
            if done:
                break
        while not side_done:
            side_done, side_val = step(side)
        return val, side_val

    blocks = [slice(i * blk, (i + 1) * blk) for i in range(tm // blk)]
    proj, _ = drive(project(blocks[0]), None)
    state = [state_ref[h] for h in range(GLA_HEADS)]
    for i, rows in enumerate(blocks):
        nxt = project(blocks[i + 1]) if i + 1 < len(blocks) else None
        state, proj = drive(scan(proj, rows, state), nxt)
    for h in range(GLA_HEADS):
        state_ref[h] = state[h]


def _gla_mixer(x2d, g, w_in, w_gk2, b_gk2, g_onorm, batch, seq):
    tm = TM_GLA
    nt = seq // tm
    n_main = 2 * GLA_QK + 2 * GLA_V
    w_pad = jnp.zeros((D_MODEL, n_main + LANES), BF16).at[:, :n_main + GLA_RANK].set(w_in.astype(BF16))
    wg2_pad = jnp.zeros((LANES, GLA_QK), BF16).at[:GLA_RANK].set(w_gk2.astype(BF16))
    row = lambda b, s: (b * nt + s, 0)
    return pl.pallas_call(
        _gla_kernel,
        grid=(batch, nt),
        in_specs=[
            pl.BlockSpec((tm, D_MODEL), row),
            _const_spec((1, D_MODEL)),
            _const_spec((D_MODEL, n_main + LANES)),
            _const_spec((LANES, GLA_QK)),
            _const_spec((1, GLA_QK)),
            _const_spec((1, GLA_DV)),
        ],
        out_specs=pl.BlockSpec((tm, GLA_V), row),
        out_shape=jax.ShapeDtypeStruct((batch * seq, GLA_V), BF16),
        scratch_shapes=[
            pltpu.VMEM((GLA_HEADS, GLA_DK, GLA_DV), F32),
            pltpu.VMEM((tm, GLA_V), F32),
        ],
        compiler_params=_params(("parallel", "arbitrary")),
        name="gla_mixer",
    )(x2d, g.reshape(1, D_MODEL), w_pad, wg2_pad, b_gk2.reshape(1, GLA_QK),
      g_onorm.reshape(1, GLA_DV))


def _ffn_kernel(x_ref, mix_ref, wmix_ref, p_ref, gf_ref, wup_ref, wdw_ref, bdw_ref, wdn_ref, gp_ref,
                wg_ref, bg_ref, wp_ref, gfin_ref, out_ref, carry_ref, act_ref, perm_ref, *, final_norm):
    tm = x_ref.shape[0]
    fc = FFN_CHUNK
    nch = D_FF // fc
    sub = SUBLANES
    span = tm // sub
    nlb = D_MODEL // LANES

    @pl.when(pl.program_id(1) == 0)
    def _():
        carry_ref[...] = jnp.zeros_like(carry_ref)

    def interleave(val):
        for lb in range(nlb):
            for s in range(sub):
                perm_ref[lb, s * FFN_PITCH:s * FFN_PITCH + span, :] = (
                    val[s * span:(s + 1) * span, lb * LANES:(lb + 1) * LANES])
        return jnp.concatenate(
            [jnp.concatenate([perm_ref[lb, pl.ds(v, sub, stride=FFN_PITCH), :] for lb in range(nlb)],
                             axis=1) for v in range(span)], axis=0)

    def deinterleave(val):
        for lb in range(nlb):
            for v in range(span):
                perm_ref[lb, pl.ds(v, sub, stride=FFN_PITCH), :] = (
                    val[v * sub:(v + 1) * sub, lb * LANES:(lb + 1) * LANES])
        return jnp.concatenate(
            [jnp.concatenate([perm_ref[lb, s * FFN_PITCH:s * FFN_PITCH + span, :] for lb in range(nlb)],
                             axis=1) for s in range(sub)], axis=0)

    x = x_ref[...] + _dot(mix_ref[...], wmix_ref[...])
    hn = interleave(_rms(x, gf_ref[...])).astype(BF16)
    rowid = lax.broadcasted_iota(jnp.int32, (sub, fc), 0)

    def wrap(cur, prev):
        return jnp.where(rowid == 0, pltpu.roll(prev, 1, 0), pltpu.roll(cur, 1, 0))

    def conv(col0, slot):
        u = _dot(hn, wup_ref[:, col0:col0 + fc])
        prev = carry_ref[slot]
        carry_ref[slot] = u[tm - 2 * sub:tm, :]
        top1 = wrap(u[tm - sub:tm, :], prev[sub:2 * sub, :])
        top2 = wrap(u[tm - 2 * sub:tm - sub, :], prev[0:sub, :])
        u1 = jnp.concatenate([top1, u[0:tm - sub, :]], axis=0)
        u2 = jnp.concatenate([top2, top1, u[0:tm - 2 * sub, :]], axis=0)
        w = wdw_ref[:, col0:col0 + fc]
        return w[0:1] * u2 + w[1:2] * u1 + w[2:3] * u + bdw_ref[:, col0:col0 + fc]

    for ci in range(nch):
        a = conv(ci * fc, 2 * ci)
        b = conv(D_FF + ci * fc, 2 * ci + 1)
        act_ref[:, ci * fc:(ci + 1) * fc] = (a * _sigmoid(a) * b).astype(BF16)
    y = x + deinterleave(_dot(act_ref[...], wdn_ref[...]))
    hp = _rms(y, gp_ref[...]).astype(BF16)
    gate = _sigmoid(_dot(hp, wg_ref[...]) + bg_ref[...])
    pe = _dot(p_ref[...].astype(BF16), wp_ref[...])
    y = y + pe * gate
    if final_norm:
        y = _rms(y, gfin_ref[...])
    out_ref[...] = y


def _ffn_ple(x2d, mix, w_mix, p3d, layer, g_ffn, w_up, w_dw, b_dw, w_down, g_ple, w_gate, b_gate,
             w_proj, g_final, batch, seq, final_norm):
    tm = TM_FFN
    nt = seq // tm
    nslots = 2 * (D_FF // FFN_CHUNK)
    row = lambda b, s: (b * nt + s, 0)
    return pl.pallas_call(
        functools.partial(_ffn_kernel, final_norm=final_norm),
        grid=(batch, nt),
        in_specs=[
            pl.BlockSpec((tm, D_MODEL), row),
            pl.BlockSpec((tm, D_MODEL), row),
            _const_spec((D_MODEL, D_MODEL)),
            pl.BlockSpec((None, tm, PLE_DIM), lambda b, s: (layer, b * nt + s, 0)),
            _const_spec((1, D_MODEL)),
            _const_spec((D_MODEL, 2 * D_FF)),
            _const_spec((3, 2 * D_FF)),
            _const_spec((1, 2 * D_FF)),
            _const_spec((D_FF, D_MODEL)),
            _const_spec((1, D_MODEL)),
            _const_spec((D_MODEL, D_MODEL)),
            _const_spec((1, D_MODEL)),
            _const_spec((PLE_DIM, D_MODEL)),
            _const_spec((1, D_MODEL)),
        ],
        out_specs=pl.BlockSpec((tm, D_MODEL), row),
        out_shape=jax.ShapeDtypeStruct(x2d.shape, F32),
        scratch_shapes=[
            pltpu.VMEM((nslots, 2 * SUBLANES, FFN_CHUNK), F32),
            pltpu.VMEM((tm, D_FF), BF16),
            pltpu.VMEM((D_MODEL // LANES, SUBLANES * FFN_PITCH, LANES), F32),
        ],
        compiler_params=_params(("parallel", "arbitrary")),
        name="ffn_ple",
    )(x2d, mix, w_mix.astype(BF16), p3d, g_ffn.reshape(1, -1), w_up.astype(BF16), w_dw,
      b_dw.reshape(1, -1), w_down.astype(BF16), g_ple.reshape(1, -1), w_gate.astype(BF16),
      b_gate.reshape(1, -1), w_proj.astype(BF16), g_final.reshape(1, -1))


def _fox_proj_kernel(x_ref, gm_ref, gkv_ref, wq_ref, wog_ref, wk_ref, wvft_ref, bf_ref,
                     gq_ref, gkn_ref, ones_ref, triu_ref,
                     q_ref, sg_ref, k_ref, vt_ref, cs_ref, ct_ref, ccol_ref):
    tm = x_ref.shape[0]

    @pl.when(pl.program_id(1) == 0)
    def _():
        ccol_ref[...] = jnp.zeros_like(ccol_ref)

    x = x_ref[...]
    xn = x * lax.rsqrt(jnp.mean(x * x, axis=-1, keepdims=True) + EPS)
    h1 = (xn * gm_ref[...]).astype(BF16)
    h2 = (xn * gkv_ref[...]).astype(BF16)

    def head_norm(t, g):
        t2 = (t * t).astype(BF16)
        blk = 2 * LANES
        ss = jnp.concatenate(
            [_dot(t2[:, i * blk:(i + 1) * blk], ones_ref[...]) for i in range(D_MODEL // blk)], axis=1)
        return t * lax.rsqrt(ss * (1.0 / FOX_HD) + EPS) * g

    q = _dot(h1, wq_ref[...])
    q_ref[...] = (head_norm(q, gq_ref[...]) * (FOX_HD ** -0.5 * LOG2E)).astype(BF16)
    sg_ref[...] = _sigmoid(_dot(h1, wog_ref[...])).astype(BF16)
    k = _dot(h2, wk_ref[...])
    k_ref[...] = head_norm(k, gkn_ref[...]).astype(BF16)
    vft = _dot_nt(wvft_ref[...], h2)
    vt_ref[...] = vft[0:D_MODEL].astype(BF16)

    lf = _log_sigmoid(vft[D_MODEL:] + bf_ref[:, 0:1])
    hi, mid, lo = _split3(lf)
    triu = triu_ref[...]
    ct = _dot(hi, triu) + _dot(mid, triu) + _dot(lo, triu) + ccol_ref[:, 0:1]
    ccol_ref[...] = jnp.broadcast_to(ct[:, tm - 1:tm], ccol_ref.shape)
    ct2 = ct * LOG2E
    ct_ref[...] = ct2
    c2 = jnp.concatenate([ct2, jnp.zeros((LANES - FOX_HEADS, tm), F32)], axis=0).T
    hi = c2.astype(BF16).astype(F32)
    r = c2 - hi
    mid = r.astype(BF16).astype(F32)
    lo = r - mid
    lane = lax.broadcasted_iota(jnp.int32, c2.shape, 1)
    cs_ref[...] = jnp.where(
        lane < FOX_HEADS, hi,
        jnp.where(lane < 2 * FOX_HEADS, pltpu.roll(mid, FOX_HEADS, 1),
                  pltpu.roll(lo, 2 * FOX_HEADS, 1))).astype(BF16)


def _fox_proj(x2d, g_mix, g_kv, w_fox_in, w_kv_in, b_f, g_qnorm, g_knorm, batch, seq):
    tm = TM_PROJ
    nt = seq // tm
    d = D_MODEL
    wq = w_fox_in[:, :d].astype(BF16)
    wog = w_fox_in[:, d:].astype(BF16)
    wk = w_kv_in[:, :d].astype(BF16)
    wvft = w_kv_in[:, d:].T.astype(BF16)
    bf_col = jnp.broadcast_to(b_f[:, None], (FOX_HEADS, LANES))
    gq = jnp.tile(g_qnorm, FOX_HEADS).reshape(1, d)
    gkn = jnp.tile(g_knorm, FOX_HEADS).reshape(1, d)
    blk = 2 * LANES
    hid = jnp.arange(blk) // FOX_HD
    ones_bd = (hid[:, None] == hid[None, :]).astype(BF16)
    tid = jnp.arange(tm)
    triu = (tid[:, None] <= tid[None, :]).astype(BF16)
    row = lambda b, s: (b * nt + s, 0)
    return pl.pallas_call(
        _fox_proj_kernel,
        grid=(batch, nt),
        in_specs=[
            pl.BlockSpec((tm, d), row),
            _const_spec((1, d)), _const_spec((1, d)),
            _const_spec((d, d)), _const_spec((d, d)), _const_spec((d, d)),
            _const_spec((d + FOX_HEADS, d)), _const_spec((FOX_HEADS, LANES)),
            _const_spec((1, d)), _const_spec((1, d)),
            _const_spec((blk, blk)), _const_spec((tm, tm)),
        ],
        out_specs=[
            pl.BlockSpec((tm, d), row),
            pl.BlockSpec((tm, d), row),
            pl.BlockSpec((tm, d), row),
            pl.BlockSpec((d, tm), lambda b, s: (b, s)),
            pl.BlockSpec((tm, LANES), row),
            pl.BlockSpec((FOX_HEADS, tm), lambda b, s: (b, s)),
        ],
        out_shape=[
            jax.ShapeDtypeStruct((batch * seq, d), BF16),
            jax.ShapeDtypeStruct((batch * seq, d), BF16),
            jax.ShapeDtypeStruct((batch * seq, d), BF16),
            jax.ShapeDtypeStruct((batch * d, seq), BF16),
            jax.ShapeDtypeStruct((batch * seq, LANES), BF16),
            jax.ShapeDtypeStruct((batch * FOX_HEADS, seq), F32),
        ],
        scratch_shapes=[pltpu.VMEM((FOX_HEADS, LANES), F32)],
        compiler_params=_params(("parallel", "arbitrary")),
        name="fox_proj",
    )(x2d, g_mix.reshape(1, d), g_kv.reshape(1, d), wq, wog, wk, wvft, bf_col, gq, gkn,
      ones_bd, triu)


def _fox_attn_kernel(q_ref, k_ref, vt_ref, cs_ref, ct_ref, sg_ref, out_ref, kaug_ref):
    seq = q_ref.shape[0]
    pair = pl.program_id(1)
    lane = lax.broadcasted_iota(jnp.int32, (TQ, LANES), 1)
    low = lane < FOX_HD
    kv_row = lax.broadcasted_iota(jnp.int32, (TK, TQ), 0)
    q_col = lax.broadcasted_iota(jnp.int32, (TK, TQ), 1)
    causal_t = jnp.concatenate([kv_row <= q_col] * 2, axis=1)

    src = lax.broadcasted_iota(jnp.int32, (LANES, LANES), 0)
    dst = lax.broadcasted_iota(jnp.int32, (LANES, LANES), 1)
    sel = jnp.zeros((LANES, LANES), F32)
    for hh in range(2):
        for term in range(3):
            hit = (dst == 3 * hh + term) & (src == FOX_HEADS * term + 2 * pair + hh)
            sel = jnp.where(hit, -1.0, sel)
    kaug_ref[:, 0:LANES] = k_ref[...]
    kaug_ref[:, LANES:] = _dot(cs_ref[...], sel.astype(BF16)).astype(BF16)
    q_ext = jnp.concatenate([jnp.where(lane < 3, 1.0, 0.0),
                             jnp.where((lane >= 3) & (lane < 6), 1.0, 0.0)], axis=0).astype(BF16)

    def pass1(qi):
        rows = slice(qi * TQ, (qi + 1) * TQ)
        qp = q_ref[rows, :]
        zero = jnp.zeros_like(qp)
        q_aug = jnp.concatenate(
            [jnp.concatenate([jnp.where(low, qp, zero), jnp.where(low, zero, qp)], axis=0), q_ext],
            axis=1)
        blocks = [[None, None] for _ in range(qi + 1)]
        mruns = []
        for hh in range(2):
            cols = slice(hh * TQ, (hh + 1) * TQ)
            st_all = _dot_nt(kaug_ref[0:(qi + 1) * TK, :], q_aug[cols])
            mrun = None
            for kb in range(qi + 1):
                krows = slice(kb * TK, (kb + 1) * TK)
                st = st_all[krows]
                if kb == qi:
                    st = jnp.where(causal_t[:, cols], st, -jnp.inf)
                blocks[kb][hh] = st
                bm = jnp.max(st.reshape(TK // SUBLANES, SUBLANES, TQ), axis=0)
                mrun = bm if mrun is None else jnp.maximum(mrun, bm)
                yield None
            mruns.append(mrun)
        cq = jnp.concatenate([ct_ref[pl.ds(2 * pair + hh, 1), rows] for hh in range(2)], axis=1)
        logit_max = jnp.max(jnp.concatenate(mruns, axis=1), axis=0, keepdims=True) + cq
        yield logit_max - cq, blocks

    def pass2(qi, shift_blocks):
        shift, blocks = shift_blocks
        rows = slice(qi * TQ, (qi + 1) * TQ)
        acc = None
        lrun = None
        for kb in range(qi + 1):
            krows = slice(kb * TK, (kb + 1) * TK)
            pt = jnp.exp2(jnp.concatenate(blocks[kb], axis=1) - shift)
            ps = jnp.sum(pt.reshape(TK // SUBLANES, SUBLANES, 2 * TQ), axis=0)
            lrun = ps if lrun is None else lrun + ps
            d = _dot(vt_ref[:, krows], pt.astype(BF16))
            acc = d if acc is None else acc + d
            yield None
        ot = acc / jnp.sum(lrun, axis=0, keepdims=True)
        o = jnp.concatenate([ot[:FOX_HD, :TQ], ot[FOX_HD:, TQ:]], axis=0).T
        out_ref[rows, :] = (o * sg_ref[rows, :].astype(F32)).astype(BF16)
        yield None

    def interleave(gen_a, gen_b):
        last_a = None
        while gen_a is not None or gen_b is not None:
            if gen_a is not None:
                try:
                    last_a = next(gen_a)
                except StopIteration:
                    gen_a = None
            if gen_b is not None:
                try:
                    next(gen_b)
                except StopIteration:
                    gen_b = None
        return last_a

    nq = seq // TQ
    pending = {qi: interleave(pass1(qi), None) for qi in range(min(ATT_AHEAD, nq))}
    for qi in range(nq):
        nxt = qi + ATT_AHEAD
        got = interleave(pass1(nxt) if nxt < nq else None, pass2(qi, pending.pop(qi)))
        if nxt < nq:
            pending[nxt] = got


def _fox_attn(q, k, vt, cs, ct, sg, batch, seq):
    d = D_MODEL
    pairs = FOX_HEADS // 2
    blk = lambda b, j: (b, j)
    return pl.pallas_call(
        _fox_attn_kernel,
        grid=(batch, pairs),
        in_specs=[
            pl.BlockSpec((seq, LANES), blk),
            pl.BlockSpec((seq, LANES), blk),
            pl.BlockSpec((LANES, seq), lambda b, j: (b * pairs + j, 0)),
            pl.BlockSpec((seq, LANES), lambda b, j: (b, 0)),
            pl.BlockSpec((FOX_HEADS, seq), lambda b, j: (b, 0)),
            pl.BlockSpec((seq, LANES), blk),
        ],
        out_specs=pl.BlockSpec((seq, LANES), blk),
        out_shape=jax.ShapeDtypeStruct((batch * seq, d), BF16),
        scratch_shapes=[pltpu.VMEM((seq, 2 * LANES), BF16)],
        compiler_params=_params(("parallel", "parallel")),
        name="fox_attn",
    )(q, k, vt, cs, ct, sg)


def kernel(x, p, g_mix, g_ffn, g_ple, gla_w_in, gla_w_gk2, gla_b_gk2, gla_g_onorm, gla_w_out,
           kv_g_norm, kv_w_in, kv_g_knorm, kv_b_f, fox_w_in, fox_g_qnorm, fox_w_out, ffn_w_up,
           ffn_w_dw, ffn_b_dw, ffn_w_down, ple_w_gate, ple_b_gate, ple_w_proj, g_final):
    batch, seq, d = x.shape
    depth = p.shape[0]
    n_gla = gla_w_in.shape[0]
    x2d = x.reshape(batch * seq, d)
    p3d = p.reshape(depth, batch * seq, PLE_DIM)
    for i in range(depth):
        if i < n_gla:
            mix = _gla_mixer(x2d, g_mix[i], gla_w_in[i], gla_w_gk2[i], gla_b_gk2[i],
                             gla_g_onorm[i], batch, seq)
            w_mix = gla_w_out[i]
        else:
            j = i - n_gla
            assert j == 0, "a single FoX layer is supported"
            q, sg, k_sh, vt_sh, cs_sh, ct_sh = _fox_proj(
                x2d, g_mix[i], kv_g_norm, fox_w_in[j], kv_w_in, kv_b_f, fox_g_qnorm[j],
                kv_g_knorm, batch, seq)
            mix = _fox_attn(q, k_sh, vt_sh, cs_sh, ct_sh, sg, batch, seq)
            w_mix = fox_w_out[j]
        x2d = _ffn_ple(x2d, mix, w_mix, p3d, i, g_ffn[i], ffn_w_up[i], ffn_w_dw[i], ffn_b_dw[i],
                       ffn_w_down[i], g_ple[i], ple_w_gate[i], ple_b_gate[i], ple_w_proj[i],
                       g_final, batch, seq, final_norm=(i == depth - 1))
    return x2d.reshape(batch, seq, d)
```

```python
import functools
import math

import jax
import jax.numpy as jnp
from jax import lax
from jax.experimental import pallas as pl
from jax.experimental.pallas import tpu as pltpu

F32 = jnp.float32
BF16 = jnp.bfloat16

EPS = 1e-6
D_MODEL = 1024
GLA_HEADS = 4
GLA_DK = 128
GLA_DV = 256
GLA_QK = GLA_HEADS * GLA_DK
GLA_V = GLA_HEADS * GLA_DV
GLA_RANK = 16
GLA_TAU = 16.0
GLA_CHUNK = 64
FOX_HEADS = 16
FOX_HD = 64
D_FF = 2816
PLE_DIM = 256
LOG2E = math.log2(math.e)

LANES = 128
SUBLANES = 8
VMEM_LIMIT = 56 * 1024 * 1024

TM_PROJ = 1024
TM_GLA = 1024
GLA_BLOCK = 512
TM_FFN = 512
FFN_CHUNK = 256
FFN_PITCH = 72
TQ = 256
TK = 256
ATT_AHEAD = 2


def _dot(a, b):
    return jnp.dot(a, b, preferred_element_type=F32)


def _dot_nt(a, b):
    return lax.dot_general(a, b, (((1,), (1,)), ((), ())), preferred_element_type=F32)


def _dot_tn(a, b):
    return lax.dot_general(a, b, (((0,), (0,)), ((), ())), preferred_element_type=F32)


def _split3(x):
    hi = x.astype(BF16)
    r = x - hi.astype(F32)
    mid = r.astype(BF16)
    lo = (r - mid.astype(F32)).astype(BF16)
    return hi, mid, lo


def _rms(x, g):
    return x * lax.rsqrt(jnp.mean(x * x, axis=-1, keepdims=True) + EPS) * g


def _sigmoid(z):
    return 1.0 / (1.0 + jnp.exp(-z))


def _log_sigmoid(z):
    return jnp.minimum(z, 0.0) - jnp.log1p(jnp.exp(-jnp.abs(z)))


def _const_spec(shape):
    nd = len(shape)
    return pl.BlockSpec(shape, lambda *_: (0,) * nd, pipeline_mode=pl.Buffered(1))


def _params(sem):
    return pltpu.CompilerParams(dimension_semantics=sem, vmem_limit_bytes=VMEM_LIMIT)


def _gla_kernel(x_ref, g_ref, w_ref, wg2_ref, bg2_ref, gon_ref, wout_ref, out_ref, state_ref, o_ref):
    c = GLA_CHUNK
    tm = x_ref.shape[0]
    blk = GLA_BLOCK

    @pl.when(pl.program_id(1) == 0)
    def _():
        state_ref[...] = jnp.zeros_like(state_ref)

    rowi = lax.broadcasted_iota(jnp.int32, (c, c), 0)
    coli = lax.broadcasted_iota(jnp.int32, (c, c), 1)
    tril = rowi >= coli
    tril_bf = jnp.where(tril, 1.0, 0.0).astype(BF16)
    scale = GLA_DK ** -0.5
    nchunk = blk // c
    chunk_rows = [slice(ci * c, (ci + 1) * c) for ci in range(nchunk)]
    head_k = [slice(h * GLA_DK, (h + 1) * GLA_DK) for h in range(GLA_HEADS)]
    head_v = [slice(h * GLA_DV, (h + 1) * GLA_DV) for h in range(GLA_HEADS)]

    def project(rows):
        hn = _rms(x_ref[rows, :], g_ref[...]).astype(BF16)
        yield
        q = _dot(hn, w_ref[:, 0:GLA_QK]).astype(BF16)
        yield
        k = _dot(hn, w_ref[:, GLA_QK:2 * GLA_QK]).astype(BF16)
        yield
        v = _dot(hn, w_ref[:, 2 * GLA_QK:2 * GLA_QK + GLA_V]).astype(BF16)
        yield
        og = _dot(hn, w_ref[:, 2 * GLA_QK + GLA_V:2 * GLA_QK + 2 * GLA_V]).astype(BF16)
        yield
        gr = _dot(hn, w_ref[:, 2 * GLA_QK + 2 * GLA_V:]).astype(BF16)
        gk = _log_sigmoid(_dot(gr, wg2_ref[...]) + bg2_ref[...]) * (1.0 / GLA_TAU)
        return q, k, v, og, gk

    def scan(proj, rows, state):
        q, k, v, og, gk = proj
        hi, mid, lo = _split3(gk)
        bcum = jnp.concatenate(
            [_dot(tril_bf, hi[r]) + _dot(tril_bf, mid[r]) + _dot(tril_bf, lo[r]) for r in chunk_rows],
            axis=0)
        yield
        b_last = [bcum[r.stop - 1:r.stop, :] for r in chunk_rows]
        b_last_rows = jnp.concatenate([jnp.broadcast_to(b, (c, GLA_QK)) for b in b_last], axis=0)
        qf = q.astype(F32) * scale
        kf = k.astype(F32)
        q_in = (qf * jnp.exp(bcum)).astype(BF16)
        yield
        k_in = (kf * jnp.exp(-bcum)).astype(BF16)
        yield
        k_end = (kf * jnp.exp(b_last_rows - bcum)).astype(BF16)
        decay = jnp.exp(jnp.concatenate(b_last, axis=0))
        pad = jnp.zeros((LANES - nchunk, GLA_DK), F32)
        decay_t = [jnp.concatenate([decay[:, ks], pad], axis=0).T for ks in head_k]
        yield
        att = [[jnp.where(tril, _dot_nt(q_in[r, ks], k_in[r, ks]), 0.0).astype(BF16)
                for ks in head_k] for r in chunk_rows]
        yield
        upd = [[_dot_tn(k_end[r, head_k[h]], v[r, head_v[h]]) for h in range(GLA_HEADS)]
               for r in chunk_rows]
        yield
        state = list(state)
        for ci, r in enumerate(chunk_rows):
            orow = slice(rows.start + r.start, rows.start + r.stop)
            for h in range(GLA_HEADS):
                o_ref[orow, head_v[h]] = (_dot(att[ci][h], v[r, head_v[h]])
                                          + _dot(q_in[r, head_k[h]], state[h].astype(BF16)))
                state[h] = decay_t[h][:, ci:ci + 1] * state[h] + upd[ci][h]
            yield
        return state, og

    def finish(rows, og):
        y = x_ref[rows, :]
        for h in range(GLA_HEADS):
            vs = head_v[h]
            on = _rms(o_ref[rows, vs], gon_ref[...])
            og_h = og[:, vs].astype(F32)
            gated = (on * (og_h * _sigmoid(og_h))).astype(BF16)
            y = y + _dot(gated, wout_ref[vs, :])
            yield
        out_ref[rows, :] = y

    def chain(first, second):
        if first is not None:
            yield from first
        if second is None:
            return None
        return (yield from second)

    def step(gen):
        try:
            next(gen)
            return False, None
        except StopIteration as stop:
            return True, stop.value

    def drive(main, side):
        side_done, side_val = (True, None) if side is None else (False, None)
        while True:
            done, val = step(main)
            if not side_done:
                side_done, side_val = step(side)
            if done:
                break
        while not side_done:
            side_done, side_val = step(side)
        return val, side_val

    blocks = [slice(i * blk, (i + 1) * blk) for i in range(tm // blk)]
    proj, _ = drive(project(blocks[0]), None)
    state = [state_ref[h] for h in range(GLA_HEADS)]
    pending = None
    for i, rows in enumerate(blocks):
        nxt = project(blocks[i + 1]) if i + 1 < len(blocks) else None
        (state, og), proj = drive(scan(proj, rows, state), chain(pending, nxt))
        pending = finish(rows, og)
    drive(pending, None)
    for h in range(GLA_HEADS):
        state_ref[h] = state[h]


def _gla_mixer(x2d, g, w_in, w_gk2, b_gk2, g_onorm, w_out, batch, seq):
    tm = TM_GLA
    nt = seq // tm
    n_main = 2 * GLA_QK + 2 * GLA_V
    w_pad = jnp.zeros((D_MODEL, n_main + LANES), BF16).at[:, :n_main + GLA_RANK].set(w_in.astype(BF16))
    wg2_pad = jnp.zeros((LANES, GLA_QK), BF16).at[:GLA_RANK].set(w_gk2.astype(BF16))
    row = lambda b, s: (b * nt + s, 0)
    return pl.pallas_call(
        _gla_kernel,
        grid=(batch, nt),
        in_specs=[
            pl.BlockSpec((tm, D_MODEL), row),
            _const_spec((1, D_MODEL)),
            _const_spec((D_MODEL, n_main + LANES)),
            _const_spec((LANES, GLA_QK)),
            _const_spec((1, GLA_QK)),
            _const_spec((1, GLA_DV)),
            _const_spec((GLA_V, D_MODEL)),
        ],
        out_specs=pl.BlockSpec((tm, D_MODEL), row),
        out_shape=jax.ShapeDtypeStruct((batch * seq, D_MODEL), F32),
        scratch_shapes=[
            pltpu.VMEM((GLA_HEADS, GLA_DK, GLA_DV), F32),
            pltpu.VMEM((tm, GLA_V), F32),
        ],
        compiler_params=_params(("parallel", "arbitrary")),
        name="gla_mixer",
    )(x2d, g.reshape(1, D_MODEL), w_pad, wg2_pad, b_gk2.reshape(1, GLA_QK),
      g_onorm.reshape(1, GLA_DV), w_out.astype(BF16))


def _ffn_kernel(x_ref, *refs, has_mix, final_norm):
    if has_mix:
        mix_ref, wmix_ref, *refs = refs
    (p_ref, gf_ref, wup_ref, wdw_ref, bdw_ref, wdn_ref, gp_ref, wg_ref, bg_ref, wp_ref, gfin_ref,
     out_ref, carry_ref, act_ref, perm_ref) = refs
    tm = x_ref.shape[0]
    fc = FFN_CHUNK
    nch = D_FF // fc
    sub = SUBLANES
    span = tm // sub
    nlb = D_MODEL // LANES

    @pl.when(pl.program_id(1) == 0)
    def _():
        carry_ref[...] = jnp.zeros_like(carry_ref)

    def interleave(val):
        for lb in range(nlb):
            for s in range(sub):
                perm_ref[lb, s * FFN_PITCH:s * FFN_PITCH + span, :] = (
                    val[s * span:(s + 1) * span, lb * LANES:(lb + 1) * LANES])
        return jnp.concatenate(
            [jnp.concatenate([perm_ref[lb, pl.ds(v, sub, stride=FFN_PITCH), :] for lb in range(nlb)],
                             axis=1) for v in range(span)], axis=0)

    def deinterleave(val):
        for lb in range(nlb):
            for v in range(span):
                perm_ref[lb, pl.ds(v, sub, stride=FFN_PITCH), :] = (
                    val[v * sub:(v + 1) * sub, lb * LANES:(lb + 1) * LANES])
        return jnp.concatenate(
            [jnp.concatenate([perm_ref[lb, s * FFN_PITCH:s * FFN_PITCH + span, :] for lb in range(nlb)],
                             axis=1) for s in range(sub)], axis=0)

    x = x_ref[...]
    if has_mix:
        x = x + _dot(mix_ref[...], wmix_ref[...])
    hn = interleave(_rms(x, gf_ref[...])).astype(BF16)
    rowid = lax.broadcasted_iota(jnp.int32, (sub, fc), 0)

    def wrap(cur, prev):
        return jnp.where(rowid == 0, pltpu.roll(prev, 1, 0), pltpu.roll(cur, 1, 0))

    def conv(col0, slot):
        u = _dot(hn, wup_ref[:, col0:col0 + fc])
        prev = carry_ref[slot]
        carry_ref[slot] = u[tm - 2 * sub:tm, :]
        top1 = wrap(u[tm - sub:tm, :], prev[sub:2 * sub, :])
        top2 = wrap(u[tm - 2 * sub:tm - sub, :], prev[0:sub, :])
        u1 = jnp.concatenate([top1, u[0:tm - sub, :]], axis=0)
        u2 = jnp.concatenate([top2, top1, u[0:tm - 2 * sub, :]], axis=0)
        w = wdw_ref[:, col0:col0 + fc]
        return w[0:1] * u2 + w[1:2] * u1 + w[2:3] * u + bdw_ref[:, col0:col0 + fc]

    for ci in range(nch):
        a = conv(ci * fc, 2 * ci)
        b = conv(D_FF + ci * fc, 2 * ci + 1)
        act_ref[:, ci * fc:(ci + 1) * fc] = (a * _sigmoid(a) * b).astype(BF16)
    y = x + deinterleave(_dot(act_ref[...], wdn_ref[...]))
    hp = _rms(y, gp_ref[...]).astype(BF16)
    gate = _sigmoid(_dot(hp, wg_ref[...]) + bg_ref[...])
    pe = _dot(p_ref[...].astype(BF16), wp_ref[...])
    y = y + pe * gate
    if final_norm:
        y = _rms(y, gfin_ref[...])
    out_ref[...] = y


def _ffn_ple(x2d, mix, w_mix, p3d, layer, g_ffn, w_up, w_dw, b_dw, w_down, g_ple, w_gate, b_gate,
             w_proj, g_final, batch, seq, final_norm):
    tm = TM_FFN
    nt = seq // tm
    nslots = 2 * (D_FF // FFN_CHUNK)
    row = lambda b, s: (b * nt + s, 0)
    has_mix = mix is not None
    mix_specs = [pl.BlockSpec((tm, D_MODEL), row), _const_spec((D_MODEL, D_MODEL))] if has_mix else []
    mix_args = (mix, w_mix.astype(BF16)) if has_mix else ()
    return pl.pallas_call(
        functools.partial(_ffn_kernel, has_mix=has_mix, final_norm=final_norm),
        grid=(batch, nt),
        in_specs=[
            pl.BlockSpec((tm, D_MODEL), row),
            *mix_specs,
            pl.BlockSpec((None, tm, PLE_DIM), lambda b, s: (layer, b * nt + s, 0)),
            _const_spec((1, D_MODEL)),
            _const_spec((D_MODEL, 2 * D_FF)),
            _const_spec((3, 2 * D_FF)),
            _const_spec((1, 2 * D_FF)),
            _const_spec((D_FF, D_MODEL)),
            _const_spec((1, D_MODEL)),
            _const_spec((D_MODEL, D_MODEL)),
            _const_spec((1, D_MODEL)),
            _const_spec((PLE_DIM, D_MODEL)),
            _const_spec((1, D_MODEL)),
        ],
        out_specs=pl.BlockSpec((tm, D_MODEL), row),
        out_shape=jax.ShapeDtypeStruct(x2d.shape, F32),
        scratch_shapes=[
            pltpu.VMEM((nslots, 2 * SUBLANES, FFN_CHUNK), F32),
            pltpu.VMEM((tm, D_FF), BF16),
            pltpu.VMEM((D_MODEL // LANES, SUBLANES * FFN_PITCH, LANES), F32),
        ],
        compiler_params=_params(("parallel", "arbitrary")),
        name="ffn_ple",
    )(x2d, *mix_args, p3d, g_ffn.reshape(1, -1), w_up.astype(BF16), w_dw,
      b_dw.reshape(1, -1), w_down.astype(BF16), g_ple.reshape(1, -1), w_gate.astype(BF16),
      b_gate.reshape(1, -1), w_proj.astype(BF16), g_final.reshape(1, -1))


def _fox_proj_kernel(x_ref, gm_ref, gkv_ref, wq_ref, wog_ref, wk_ref, wvft_ref, bf_ref,
                     gq_ref, gkn_ref, ones_ref, triu_ref,
                     q_ref, sg_ref, k_ref, vt_ref, cs_ref, ct_ref, ccol_ref):
    tm = x_ref.shape[0]

    @pl.when(pl.program_id(1) == 0)
    def _():
        ccol_ref[...] = jnp.zeros_like(ccol_ref)

    x = x_ref[...]
    xn = x * lax.rsqrt(jnp.mean(x * x, axis=-1, keepdims=True) + EPS)
    h1 = (xn * gm_ref[...]).astype(BF16)
    h2 = (xn * gkv_ref[...]).astype(BF16)

    def head_norm(t, g):
        t2 = (t * t).astype(BF16)
        blk = 2 * LANES
        ss = jnp.concatenate(
            [_dot(t2[:, i * blk:(i + 1) * blk], ones_ref[...]) for i in range(D_MODEL // blk)], axis=1)
        return t * lax.rsqrt(ss * (1.0 / FOX_HD) + EPS) * g

    q = _dot(h1, wq_ref[...])
    q_ref[...] = (head_norm(q, gq_ref[...]) * (FOX_HD ** -0.5 * LOG2E)).astype(BF16)
    sg_ref[...] = _sigmoid(_dot(h1, wog_ref[...])).astype(BF16)
    k = _dot(h2, wk_ref[...])
    k_ref[...] = head_norm(k, gkn_ref[...]).astype(BF16)
    vft = _dot_nt(wvft_ref[...], h2)
    vt_ref[...] = vft[0:D_MODEL].astype(BF16)

    lf = _log_sigmoid(vft[D_MODEL:] + bf_ref[:, 0:1])
    hi, mid, lo = _split3(lf)
    triu = triu_ref[...]
    ct = _dot(hi, triu) + _dot(mid, triu) + _dot(lo, triu) + ccol_ref[:, 0:1]
    ccol_ref[...] = jnp.broadcast_to(ct[:, tm - 1:tm], ccol_ref.shape)
    ct2 = ct * LOG2E
    ct_ref[...] = ct2
    c2 = jnp.concatenate([ct2, jnp.zeros((LANES - FOX_HEADS, tm), F32)], axis=0).T
    hi = c2.astype(BF16).astype(F32)
    r = c2 - hi
    mid = r.astype(BF16).astype(F32)
    lo = r - mid
    lane = lax.broadcasted_iota(jnp.int32, c2.shape, 1)
    cs_ref[...] = jnp.where(
        lane < FOX_HEADS, hi,
        jnp.where(lane < 2 * FOX_HEADS, pltpu.roll(mid, FOX_HEADS, 1),
                  pltpu.roll(lo, 2 * FOX_HEADS, 1))).astype(BF16)


def _fox_proj(x2d, g_mix, g_kv, w_fox_in, w_kv_in, b_f, g_qnorm, g_knorm, batch, seq):
    tm = TM_PROJ
    nt = seq // tm
    d = D_MODEL
    wq = w_fox_in[:, :d].astype(BF16)
    wog = w_fox_in[:, d:].astype(BF16)
    wk = w_kv_in[:, :d].astype(BF16)
    wvft = w_kv_in[:, d:].T.astype(BF16)
    bf_col = jnp.broadcast_to(b_f[:, None], (FOX_HEADS, LANES))
    gq = jnp.tile(g_qnorm, FOX_HEADS).reshape(1, d)
    gkn = jnp.tile(g_knorm, FOX_HEADS).reshape(1, d)
    blk = 2 * LANES
    hid = jnp.arange(blk) // FOX_HD
    ones_bd = (hid[:, None] == hid[None, :]).astype(BF16)
    tid = jnp.arange(tm)
    triu = (tid[:, None] <= tid[None, :]).astype(BF16)
    row = lambda b, s: (b * nt + s, 0)
    return pl.pallas_call(
        _fox_proj_kernel,
        grid=(batch, nt),
        in_specs=[
            pl.BlockSpec((tm, d), row),
            _const_spec((1, d)), _const_spec((1, d)),
            _const_spec((d, d)), _const_spec((d, d)), _const_spec((d, d)),
            _const_spec((d + FOX_HEADS, d)), _const_spec((FOX_HEADS, LANES)),
            _const_spec((1, d)), _const_spec((1, d)),
            _const_spec((blk, blk)), _const_spec((tm, tm)),
        ],
        out_specs=[
            pl.BlockSpec((tm, d), row),
            pl.BlockSpec((tm, d), row),
            pl.BlockSpec((tm, d), row),
            pl.BlockSpec((d, tm), lambda b, s: (b, s)),
            pl.BlockSpec((tm, LANES), row),
            pl.BlockSpec((FOX_HEADS, tm), lambda b, s: (b, s)),
        ],
        out_shape=[
            jax.ShapeDtypeStruct((batch * seq, d), BF16),
            jax.ShapeDtypeStruct((batch * seq, d), BF16),
            jax.ShapeDtypeStruct((batch * seq, d), BF16),
            jax.ShapeDtypeStruct((batch * d, seq), BF16),
            jax.ShapeDtypeStruct((batch * seq, LANES), BF16),
            jax.ShapeDtypeStruct((batch * FOX_HEADS, seq), F32),
        ],
        scratch_shapes=[pltpu.VMEM((FOX_HEADS, LANES), F32)],
        compiler_params=_params(("parallel", "arbitrary")),
        name="fox_proj",
    )(x2d, g_mix.reshape(1, d), g_kv.reshape(1, d), wq, wog, wk, wvft, bf_col, gq, gkn,
      ones_bd, triu)


def _fox_attn_kernel(q_ref, k_ref, vt_ref, cs_ref, ct_ref, sg_ref, out_ref, kaug_ref):
    seq = q_ref.shape[0]
    pair = pl.program_id(1)
    lane = lax.broadcasted_iota(jnp.int32, (TQ, LANES), 1)
    low = lane < FOX_HD
    kv_row = lax.broadcasted_iota(jnp.int32, (TK, TQ), 0)
    q_col = lax.broadcasted_iota(jnp.int32, (TK, TQ), 1)
    causal_t = jnp.concatenate([kv_row <= q_col] * 2, axis=1)

    src = lax.broadcasted_iota(jnp.int32, (LANES, LANES), 0)
    dst = lax.broadcasted_iota(jnp.int32, (LANES, LANES), 1)
    sel = jnp.zeros((LANES, LANES), F32)
    for hh in range(2):
        for term in range(3):
            hit = (dst == 3 * hh + term) & (src == FOX_HEADS * term + 2 * pair + hh)
            sel = jnp.where(hit, -1.0, sel)
    kaug_ref[:, 0:LANES] = k_ref[...]
    kaug_ref[:, LANES:] = _dot(cs_ref[...], sel.astype(BF16)).astype(BF16)
    q_ext = jnp.concatenate([jnp.where(lane < 3, 1.0, 0.0),
                             jnp.where((lane >= 3) & (lane < 6), 1.0, 0.0)], axis=0).astype(BF16)

    def pass1(qi):
        rows = slice(qi * TQ, (qi + 1) * TQ)
        qp = q_ref[rows, :]
        zero = jnp.zeros_like(qp)
        q_aug = jnp.concatenate(
            [jnp.concatenate([jnp.where(low, qp, zero), jnp.where(low, zero, qp)], axis=0), q_ext],
            axis=1)
        blocks = [[None, None] for _ in range(qi + 1)]
        mruns = []
        for hh in range(2):
            cols = slice(hh * TQ, (hh + 1) * TQ)
            st_all = _dot_nt(kaug_ref[0:(qi + 1) * TK, :], q_aug[cols])
            mrun = None
            for kb in range(qi + 1):
                krows = slice(kb * TK, (kb + 1) * TK)
                st = st_all[krows]
                if kb == qi:
                    st = jnp.where(causal_t[:, cols], st, -jnp.inf)
                blocks[kb][hh] = st
                bm = jnp.max(st.reshape(TK // SUBLANES, SUBLANES, TQ), axis=0)
                mrun = bm if mrun is None else jnp.maximum(mrun, bm)
                yield None
            mruns.append(mrun)
        cq = jnp.concatenate([ct_ref[pl.ds(2 * pair + hh, 1), rows] for hh in range(2)], axis=1)
        logit_max = jnp.max(jnp.concatenate(mruns, axis=1), axis=0, keepdims=True) + cq
        yield logit_max - cq, blocks

    def pass2(qi, shift_blocks):
        shift, blocks = shift_blocks
        rows = slice(qi * TQ, (qi + 1) * TQ)
        acc = None
        lrun = None
        for kb in range(qi + 1):
            krows = slice(kb * TK, (kb + 1) * TK)
            pt = jnp.exp2(jnp.concatenate(blocks[kb], axis=1) - shift)
            ps = jnp.sum(pt.reshape(TK // SUBLANES, SUBLANES, 2 * TQ), axis=0)
            lrun = ps if lrun is None else lrun + ps
            d = _dot(vt_ref[:, krows], pt.astype(BF16))
            acc = d if acc is None else acc + d
            yield None
        ot = acc / jnp.sum(lrun, axis=0, keepdims=True)
        o = jnp.concatenate([ot[:FOX_HD, :TQ], ot[FOX_HD:, TQ:]], axis=0).T
        out_ref[rows, :] = (o * sg_ref[rows, :].astype(F32)).astype(BF16)
        yield None

    def interleave(gen_a, gen_b):
        last_a = None
        while gen_a is not None or gen_b is not None:
            if gen_a is not None:
                try:
                    last_a = next(gen_a)
                except StopIteration:
                    gen_a = None
            if gen_b is not None:
                try:
                    next(gen_b)
                except StopIteration:
                    gen_b = None
        return last_a

    nq = seq // TQ
    pending = {qi: interleave(pass1(qi), None) for qi in range(min(ATT_AHEAD, nq))}
    for qi in range(nq):
        nxt = qi + ATT_AHEAD
        got = interleave(pass1(nxt) if nxt < nq else None, pass2(qi, pending.pop(qi)))
        if nxt < nq:
            pending[nxt] = got


def _fox_attn(q, k, vt, cs, ct, sg, batch, seq):
    d = D_MODEL
    pairs = FOX_HEADS // 2
    blk = lambda b, j: (b, j)
    return pl.pallas_call(
        _fox_attn_kernel,
        grid=(batch, pairs),
        in_specs=[
            pl.BlockSpec((seq, LANES), blk),
            pl.BlockSpec((seq, LANES), blk),
            pl.BlockSpec((LANES, seq), lambda b, j: (b * pairs + j, 0)),
            pl.BlockSpec((seq, LANES), lambda b, j: (b, 0)),
            pl.BlockSpec((FOX_HEADS, seq), lambda b, j: (b, 0)),
            pl.BlockSpec((seq, LANES), blk),
        ],
        out_specs=pl.BlockSpec((seq, LANES), blk),
        out_shape=jax.ShapeDtypeStruct((batch * seq, d), BF16),
        scratch_shapes=[pltpu.VMEM((seq, 2 * LANES), BF16)],
        compiler_params=_params(("parallel", "parallel")),
        name="fox_attn",
    )(q, k, vt, cs, ct, sg)


def kernel(x, p, g_mix, g_ffn, g_ple, gla_w_in, gla_w_gk2, gla_b_gk2, gla_g_onorm, gla_w_out,
           kv_g_norm, kv_w_in, kv_g_knorm, kv_b_f, fox_w_in, fox_g_qnorm, fox_w_out, ffn_w_up,
           ffn_w_dw, ffn_b_dw, ffn_w_down, ple_w_gate, ple_b_gate, ple_w_proj, g_final):
    batch, seq, d = x.shape
    depth = p.shape[0]
    n_gla = gla_w_in.shape[0]
    x2d = x.reshape(batch * seq, d)
    p3d = p.reshape(depth, batch * seq, PLE_DIM)
    for i in range(depth):
        if i < n_gla:
            x2d = _gla_mixer(x2d, g_mix[i], gla_w_in[i], gla_w_gk2[i], gla_b_gk2[i],
                             gla_g_onorm[i], gla_w_out[i], batch, seq)
            mix = w_mix = None
        else:
            j = i - n_gla
            assert j == 0, "a single FoX layer is supported"
            q, sg, k_sh, vt_sh, cs_sh, ct_sh = _fox_proj(
                x2d, g_mix[i], kv_g_norm, fox_w_in[j], kv_w_in, kv_b_f, fox_g_qnorm[j],
                kv_g_knorm, batch, seq)
            mix = _fox_attn(q, k_sh, vt_sh, cs_sh, ct_sh, sg, batch, seq)
            w_mix = fox_w_out[j]
        x2d = _ffn_ple(x2d, mix, w_mix, p3d, i, g_ffn[i], ffn_w_up[i], ffn_w_dw[i], ffn_b_dw[i],
                       ffn_w_down[i], g_ple[i], ple_w_gate[i], ple_b_gate[i], ple_w_proj[i],
                       g_final, batch, seq, final_norm=(i == depth - 1))
    return x2d.reshape(batch, seq, d)
```
